```python
import jax, jax.numpy as jnp
from jax import lax
import numpy as np

D_MODEL = 1024
BATCH = 8
SEQ = 2048
DEPTH = 1
DEC_BATCH = 128
DEC_SEQ = 1
PAST_LEN = 16384
PAGE_SIZE = 128

D_A = D_MODEL
D_B = D_MODEL
N_MEM = 256
N_XHEADS = 4
D_XHEAD = D_MODEL // N_XHEADS
D_X = N_XHEADS * D_XHEAD
N_BRANCH = 3
K_A = 3
K_B = 31
D_FF = -(-8 * D_MODEL // (3 * 256)) * 256
SPLIT_SIZES = (D_A, D_A, D_A, D_B, D_B, D_X, N_BRANCH * D_MODEL)
D_IN = sum(SPLIT_SIZES)
EPS = 1e-6

kernel_name = "gated_branch_conv_xattn_decoder_step"


def rmsnorm(x, g):
    x32 = x.astype(jnp.float32)
    y = x32 * lax.rsqrt(jnp.mean(x32 * x32, axis=-1, keepdims=True) + EPS)
    return y.astype(x.dtype) * g


def layernorm(x, g, b):
    x32 = x.astype(jnp.float32)
    mu = jnp.mean(x32, axis=-1, keepdims=True)
    var = jnp.mean(jnp.square(x32 - mu), axis=-1, keepdims=True)
    y = (x32 - mu) * lax.rsqrt(var + EPS)
    return y.astype(x.dtype) * g + b


def causal_dwconv(buf, u, w):
    seq = jnp.concatenate([buf.astype(u.dtype), u], axis=1)
    taps = w.shape[0]
    out = lax.conv_general_dilated(
        seq, w.astype(seq.dtype)[:, None, :], window_strides=(1,), padding="VALID",
        dimension_numbers=("NWC", "WIO", "NWC"), feature_group_count=u.shape[-1])
    return out, seq[:, seq.shape[1] - (taps - 1):, :]


def mem_kv(mem, norm_mem, w_k, w_v):
    m = rmsnorm(mem, norm_mem)
    k = jnp.einsum("bmd,dhe->bmhe", m, w_k)
    v = jnp.einsum("bmd,dhe->bmhe", m, w_v)
    return k, v


def layer(x, buf_a, buf_b, mem_k, mem_v, norm_mix, w_in, b_gate, conv_a_w, w_out_a,
          conv_b_w, conv_b_bias, ln_b_g, ln_b_b, w_out_b, w_out_x, w_o,
          norm_ffn, w_ff_gate, w_ff_up, w_ff_down):
    bsz, t, _ = x.shape
    xn = rmsnorm(x, norm_mix)
    proj = xn @ w_in
    cuts = [int(c) for c in np.cumsum(SPLIT_SIZES)[:-1]]
    a_b, a_c, a_x, b_val, b_glu, q, gate_logits = jnp.split(proj, cuts, axis=-1)

    ya, new_a = causal_dwconv(buf_a, a_c * a_x, conv_a_w)
    ya = (a_b * ya) @ w_out_a

    ub = b_val * jax.nn.sigmoid(b_glu)
    yb, new_b = causal_dwconv(buf_b, ub, conv_b_w)
    yb = jax.nn.silu(layernorm(yb + conv_b_bias, ln_b_g, ln_b_b)) @ w_out_b

    qh = q.reshape(bsz, t, N_XHEADS, D_XHEAD)
    s = jnp.einsum("bthe,bmhe->bhtm", qh, mem_k.astype(qh.dtype)).astype(jnp.float32)
    p = jax.nn.softmax(s * (D_XHEAD ** -0.5), axis=-1).astype(x.dtype)
    o = jnp.einsum("bhtm,bmhe->bthe", p, mem_v.astype(x.dtype)).reshape(bsz, t, D_X)
    yx = o @ w_out_x

    g = jax.nn.sigmoid(gate_logits + b_gate).reshape(bsz, t, N_BRANCH, D_MODEL)
    merged = g[:, :, 0] * ya + g[:, :, 1] * yb + g[:, :, 2] * yx
    h = x + merged @ w_o

    hn = rmsnorm(h, norm_ffn)
    h = h + (jax.nn.silu(hn @ w_ff_gate) * (hn @ w_ff_up)) @ w_ff_down
    return h, new_a, new_b


def setup_inputs(seed: int = 0) -> dict:
    key = jax.random.key(seed)
    ks = jax.random.split(key, 32)
    f32 = jnp.float32

    def nrm(k, shape, scale):
        return jax.random.normal(k, shape, f32) * scale

    def gain(k, shape):
        return 1.0 + 0.05 * jax.random.normal(k, shape, f32)

    L = DEPTH
    return {
        "x_prompt": nrm(ks[0], (BATCH, SEQ, D_MODEL), 1.0),
        "x_sample": nrm(ks[1], (DEC_BATCH, DEC_SEQ, D_MODEL), 1.0),
        "mem_prompt": nrm(ks[2], (BATCH, N_MEM, D_MODEL), 1.0),
        "cache_mem_k": nrm(ks[3], (L, DEC_BATCH, N_MEM, N_XHEADS, D_XHEAD), 1.0),
        "cache_mem_v": nrm(ks[4], (L, DEC_BATCH, N_MEM, N_XHEADS, D_XHEAD), 1.0),
        "state_conv_a": nrm(ks[5], (L, DEC_BATCH, K_A - 1, D_A), 1.0),
        "state_conv_b": nrm(ks[6], (L, DEC_BATCH, K_B - 1, D_B), 1.0),
        "norm_mix": gain(ks[7], (L, D_MODEL)),
        "w_in": nrm(ks[8], (L, D_MODEL, D_IN), D_MODEL ** -0.5),
        "b_gate": nrm(ks[9], (L, N_BRANCH * D_MODEL), 0.1),
        "conv_a_w": nrm(ks[10], (L, K_A, D_A), K_A ** -0.5),
        "w_out_a": nrm(ks[11], (L, D_A, D_MODEL), D_A ** -0.5),
        "conv_b_w": nrm(ks[12], (L, K_B, D_B), K_B ** -0.5),
        "conv_b_bias": nrm(ks[13], (L, D_B), 0.1),
        "ln_b_g": gain(ks[14], (L, D_B)),
        "ln_b_b": nrm(ks[15], (L, D_B), 0.1),
        "w_out_b": nrm(ks[16], (L, D_B, D_MODEL), D_B ** -0.5),
        "norm_mem": gain(ks[17], (L, D_MODEL)),
        "w_k": nrm(ks[18], (L, D_MODEL, N_XHEADS, D_XHEAD), D_MODEL ** -0.5),
        "w_v": nrm(ks[19], (L, D_MODEL, N_XHEADS, D_XHEAD), D_MODEL ** -0.5),
        "w_out_x": nrm(ks[20], (L, D_X, D_MODEL), D_X ** -0.5),
        "w_o": nrm(ks[21], (L, D_MODEL, D_MODEL), D_MODEL ** -0.5),
        "norm_ffn": gain(ks[22], (L, D_MODEL)),
        "w_ff_gate": nrm(ks[23], (L, D_MODEL, D_FF), D_MODEL ** -0.5),
        "w_ff_up": nrm(ks[24], (L, D_MODEL, D_FF), D_MODEL ** -0.5),
        "w_ff_down": nrm(ks[25], (L, D_FF, D_MODEL), D_FF ** -0.5),
        "norm_final": gain(ks[26], (D_MODEL,)),
    }


def reference(x_prompt, x_sample, mem_prompt, cache_mem_k, cache_mem_v, state_conv_a,
              state_conv_b, norm_mix, w_in, b_gate, conv_a_w, w_out_a, conv_b_w,
              conv_b_bias, ln_b_g, ln_b_b, w_out_b, norm_mem, w_k, w_v, w_out_x, w_o,
              norm_ffn, w_ff_gate, w_ff_up, w_ff_down, norm_final):
    hp, hs = x_prompt, x_sample
    bp = x_prompt.shape[0]
    mk_p, mv_p, ca_p, cb_p, ca_s, cb_s = [], [], [], [], [], []
    for l in range(DEPTH):
        lp = (norm_mix[l], w_in[l], b_gate[l], conv_a_w[l], w_out_a[l], conv_b_w[l],
              conv_b_bias[l], ln_b_g[l], ln_b_b[l], w_out_b[l], w_out_x[l], w_o[l],
              norm_ffn[l], w_ff_gate[l], w_ff_up[l], w_ff_down[l])
        k_p, v_p = mem_kv(mem_prompt, norm_mem[l], w_k[l], w_v[l])
        zero_a = jnp.zeros((bp, K_A - 1, D_A), hp.dtype)
        zero_b = jnp.zeros((bp, K_B - 1, D_B), hp.dtype)
        hp, na_p, nb_p = layer(hp, zero_a, zero_b, k_p, v_p, *lp)
        hs, na_s, nb_s = layer(hs, state_conv_a[l], state_conv_b[l],
                               cache_mem_k[l], cache_mem_v[l], *lp)
        mk_p.append(k_p); mv_p.append(v_p)
        ca_p.append(na_p); cb_p.append(nb_p)
        ca_s.append(na_s); cb_s.append(nb_s)
    y_prompt = rmsnorm(hp, norm_final)
    y_sample = rmsnorm(hs, norm_final)
    return (y_prompt, y_sample, jnp.stack(mk_p), jnp.stack(mv_p), jnp.stack(ca_p),
            jnp.stack(cb_p), jnp.stack(ca_s), jnp.stack(cb_s))
```

```python
import functools

import jax
import jax.numpy as jnp
from jax import lax
from jax.experimental import pallas as pl
from jax.experimental.pallas import tpu as pltpu

F32 = jnp.float32
BF16 = jnp.bfloat16
EPS = 1e-6

K_A = 3
K_B = 31
N_HEADS = 4
SUBLANES = 8
LANES = 128
HALO_A = 8
HALO_B = 32
CONV_ROWS = 32

PROMPT_TILE = 512
SAMPLE_TOKENS = 8
MIB = 1024 * 1024


def _dot(a, b):
    return jnp.dot(a, b, preferred_element_type=F32)


def _rms(x, g):
    ms = jnp.mean(x * x, axis=-1, keepdims=True)
    return x * lax.rsqrt(ms + EPS) * g


def _sigmoid(x):
    return 1.0 / (1.0 + jnp.exp(-x))


def _layernorm_swish(y, g, b):
    mu = jnp.mean(y, axis=-1, keepdims=True)
    yc = y - mu
    var = jnp.mean(yc * yc, axis=-1, keepdims=True)
    z = yc * lax.rsqrt(var + EPS) * g + b
    return z * _sigmoid(z)


def _resident(shape):
    nd = len(shape)
    return pl.BlockSpec(shape, lambda *_: (0,) * nd, pipeline_mode=pl.Buffered(1))


def _kv_kernel(mem_ref, g_ref, wk_ref, wv_ref, wkt_ref, k_ref, v_ref, kt_ref, vb_ref):
    mn = _rms(mem_ref[0], g_ref[...]).astype(BF16)
    k_ref[0] = _dot(mn, wk_ref[...])
    v = _dot(mn, wv_ref[...])
    v_ref[0] = v
    vb_ref[0] = v.astype(BF16)
    kt = lax.dot_general(wkt_ref[...], mn, (((1,), (1,)), ((), ())),
                         preferred_element_type=F32)
    kt_ref[0] = kt.astype(BF16)


def _kv_call(mem, g, wk, wv, wkt):
    b, n_mem, d = mem.shape
    dx = wk.shape[1]
    return pl.pallas_call(
        _kv_kernel,
        grid=(b,),
        in_specs=[
            pl.BlockSpec((1, n_mem, d), lambda i: (i, 0, 0)),
            _resident((1, d)),
            _resident((d, dx)),
            _resident((d, dx)),
            _resident((dx, d)),
        ],
        out_specs=[
            pl.BlockSpec((1, n_mem, dx), lambda i: (i, 0, 0)),
            pl.BlockSpec((1, n_mem, dx), lambda i: (i, 0, 0)),
            pl.BlockSpec((1, dx, n_mem), lambda i: (i, 0, 0)),
            pl.BlockSpec((1, n_mem, dx), lambda i: (i, 0, 0)),
        ],
        out_shape=[
            jax.ShapeDtypeStruct((b, n_mem, dx), F32),
            jax.ShapeDtypeStruct((b, n_mem, dx), F32),
            jax.ShapeDtypeStruct((b, dx, n_mem), BF16),
            jax.ShapeDtypeStruct((b, n_mem, dx), BF16),
        ],
        compiler_params=pltpu.CompilerParams(
            dimension_semantics=("arbitrary",), vmem_limit_bytes=32 * MIB),
        name="kv",
    )(mem, g, wk, wv, wkt)


def _mixer_kernel(x_ref, kt_ref, vb_ref, w_in_ref, woa_ref, wob_ref, wox_ref, wo_ref,
                  nmix_ref, bgate_ref, caw_ref, cbw_ref, cbb_ref, lng_ref, lnb_ref,
                  h_ref, na_ref, nb_ref, ua_buf, ub_buf, preb_buf, *, tm, d):
    t = pl.program_id(1)
    dh = d // N_HEADS

    @pl.when(t == 0)
    def _():
        ua_buf[0:HALO_A, :] = jnp.zeros((HALO_A, d), F32)
        ub_buf[0:HALO_B, :] = jnp.zeros((HALO_B, d), F32)

    x = x_ref[0]
    xn = _rms(x, nmix_ref[...]).astype(BF16)

    def proj(c0, c1):
        return _dot(xn, w_in_ref[:, c0 * d:c1 * d])

    cx = proj(1, 3)
    ua_buf[HALO_A:HALO_A + tm, :] = cx[:, :d] * cx[:, d:]
    conv_a = caw_ref[0:1, :] * ua_buf[HALO_A - 2:HALO_A - 2 + tm, :]
    conv_a = conv_a + caw_ref[1:2, :] * ua_buf[HALO_A - 1:HALO_A - 1 + tm, :]
    conv_a = conv_a + caw_ref[2:3, :] * ua_buf[HALO_A:HALO_A + tm, :]
    ya = _dot((proj(0, 1) * conv_a).astype(BF16), woa_ref[...])
    na_ref[0] = ua_buf[HALO_A + tm - (K_A - 1):HALO_A + tm, :]
    ua_buf[0:HALO_A, :] = ua_buf[tm:tm + HALO_A, :]

    vg = proj(3, 5)
    ub_buf[HALO_B:HALO_B + tm, :] = vg[:, :d] * _sigmoid(vg[:, d:])

    def conv_rows(i, carry):
        r0 = pl.multiple_of(i * CONV_ROWS, CONV_ROWS)
        win_rows = CONV_ROWS + HALO_B
        blocks = []
        for lb in range(d // LANES):
            lanes = slice(lb * LANES, (lb + 1) * LANES)
            win = ub_buf[pl.ds(r0, win_rows), lanes]
            acc = jnp.zeros((CONV_ROWS, LANES), F32)
            for r in range(SUBLANES):
                rolled = win if r == 0 else pltpu.roll(win, win_rows - r, 0)
                for a in range(win_rows // SUBLANES):
                    j = SUBLANES * a + r - (HALO_B - (K_B - 1))
                    if 0 <= j < K_B:
                        rows = rolled[SUBLANES * a:SUBLANES * a + CONV_ROWS, :]
                        acc = acc + cbw_ref[j:j + 1, lanes] * rows
            blocks.append(acc)
        acc = jnp.concatenate(blocks, axis=-1)
        z = _layernorm_swish(acc + cbb_ref[...], lng_ref[...], lnb_ref[...])
        preb_buf[pl.ds(r0, CONV_ROWS), :] = z.astype(BF16)
        return carry

    lax.fori_loop(0, tm // CONV_ROWS, conv_rows, 0)
    yb = _dot(preb_buf[...], wob_ref[...])
    nb_ref[0] = ub_buf[HALO_B + tm - (K_B - 1):HALO_B + tm, :]
    ub_buf[0:HALO_B, :] = ub_buf[tm:tm + HALO_B, :]

    q = (proj(5, 6) * (dh ** -0.5)).astype(BF16)
    heads = []
    for hd in range(N_HEADS):
        sl = slice(hd * dh, (hd + 1) * dh)
        s = _dot(q[:, sl], kt_ref[0, sl, :])
        e = jnp.exp(s - jnp.max(s, axis=-1, keepdims=True))
        p = e * (1.0 / jnp.sum(e, axis=-1, keepdims=True))
        heads.append(_dot(p.astype(BF16), vb_ref[0, :, sl]))
    yx = _dot(jnp.concatenate(heads, axis=-1).astype(BF16), wox_ref[...])

    g = _sigmoid(proj(6, 9) + bgate_ref[...])
    merged = g[:, :d] * ya + g[:, d:2 * d] * yb + g[:, 2 * d:] * yx
    h_ref[0] = x + _dot(merged.astype(BF16), wo_ref[...])


def _mixer_call(x, kt, vb, w_in, woa, wob, wox, wo, nmix, bgate, caw, cbw, cbb, lng, lnb):
    b, t, d = x.shape
    tm = PROMPT_TILE
    n_mem = vb.shape[1]
    d_in = w_in.shape[1]
    kernel = functools.partial(_mixer_kernel, tm=tm, d=d)
    return pl.pallas_call(
        kernel,
        grid=(b, t // tm),
        in_specs=[
            pl.BlockSpec((1, tm, d), lambda i, j: (i, j, 0)),
            pl.BlockSpec((1, d, n_mem), lambda i, j: (i, 0, 0)),
            pl.BlockSpec((1, n_mem, d), lambda i, j: (i, 0, 0)),
            _resident((d, d_in)),
            _resident((d, d)), _resident((d, d)), _resident((d, d)), _resident((d, d)),
            _resident((1, d)),
            _resident((1, 3 * d)),
            _resident((K_A, d)),
            _resident((K_B, d)),
            _resident((1, d)), _resident((1, d)), _resident((1, d)),
        ],
        out_specs=[
            pl.BlockSpec((1, tm, d), lambda i, j: (i, j, 0)),
            pl.BlockSpec((1, K_A - 1, d), lambda i, j: (i, 0, 0)),
            pl.BlockSpec((1, K_B - 1, d), lambda i, j: (i, 0, 0)),
        ],
        out_shape=[
            jax.ShapeDtypeStruct((b, t, d), F32),
            jax.ShapeDtypeStruct((b, K_A - 1, d), F32),
            jax.ShapeDtypeStruct((b, K_B - 1, d), F32),
        ],
        scratch_shapes=[
            pltpu.VMEM((HALO_A + tm, d), F32),
            pltpu.VMEM((HALO_B + tm, d), F32),
            pltpu.VMEM((tm, d), BF16),
        ],
        compiler_params=pltpu.CompilerParams(
            dimension_semantics=("arbitrary", "arbitrary"), vmem_limit_bytes=58 * MIB),
        name="mixer",
    )(x, kt, vb, w_in, woa, wob, wox, wo, nmix, bgate, caw, cbw, cbb, lng, lnb)


def _ffn_kernel(h_ref, nffn_ref, wg_ref, wu_ref, wd_ref, nfin_ref, y_ref):
    h = h_ref[...]
    hn = _rms(h, nffn_ref[...]).astype(BF16)
    gate = _dot(hn, wg_ref[...])
    up = _dot(hn, wu_ref[...])
    act = (gate * _sigmoid(gate) * up).astype(BF16)
    h2 = h + _dot(act, wd_ref[...])
    y_ref[...] = _rms(h2, nfin_ref[...])


def _ffn_call(h, nffn, wg, wu, wd, nfin, tm):
    m, d = h.shape
    dff = wg.shape[1]
    return pl.pallas_call(
        _ffn_kernel,
        grid=(m // tm,),
        in_specs=[
            pl.BlockSpec((tm, d), lambda i: (i, 0)),
            _resident((1, d)),
            _resident((d, dff)), _resident((d, dff)), _resident((dff, d)),
            _resident((1, d)),
        ],
        out_specs=pl.BlockSpec((tm, d), lambda i: (i, 0)),
        out_shape=jax.ShapeDtypeStruct((m, d), F32),
        compiler_params=pltpu.CompilerParams(
            dimension_semantics=("arbitrary",), vmem_limit_bytes=56 * MIB),
        name="ffn",
    )(h, nffn, wg, wu, wd, nfin)


def _sproj_kernel(x_ref, nmix_ref, w_ref, o_ref):
    xn = _rms(x_ref[...], nmix_ref[...]).astype(BF16)
    o_ref[...] = _dot(xn, w_ref[...])


def _sproj_call(x, nmix, w_in):
    m, d = x.shape
    d_in = w_in.shape[1]
    return pl.pallas_call(
        _sproj_kernel,
        grid=(d_in // d,),
        in_specs=[
            _resident((m, d)),
            _resident((1, d)),
            pl.BlockSpec((d, d), lambda j: (0, j)),
        ],
        out_specs=pl.BlockSpec((m, d), lambda j: (0, j)),
        out_shape=jax.ShapeDtypeStruct((m, d_in), F32),
        compiler_params=pltpu.CompilerParams(
            dimension_semantics=("arbitrary",), vmem_limit_bytes=32 * MIB),
        name="sample_proj",
    )(x, nmix, w_in)


def _stoken_kernel(pr_ref, sa_ref, sb_ref, k_ref, v_ref, caw_ref, cbw_ref, cbb_ref,
                   lng_ref, lnb_ref, prea_ref, preb_ref, o_ref, na_ref, nb_ref, *, tb, d):
    dh = d // N_HEADS
    a_b = pr_ref[:, 0:d]
    ua = pr_ref[:, d:2 * d] * pr_ref[:, 2 * d:3 * d]
    ub = pr_ref[:, 3 * d:4 * d] * _sigmoid(pr_ref[:, 4 * d:5 * d])
    q = pr_ref[:, 5 * d:6 * d] * (dh ** -0.5)
    ones = jnp.ones((dh, dh), BF16)
    for i in range(tb):
        row = slice(i, i + 1)
        conv_a = (caw_ref[0:1, :] * sa_ref[i, 0:1, :] + caw_ref[1:2, :] * sa_ref[i, 1:2, :]
                  + caw_ref[2:3, :] * ua[row])
        prea_ref[row, :] = a_b[row] * conv_a
        na_ref[i, 0:1, :] = sa_ref[i, 1:2, :]
        na_ref[i, 1:2, :] = ua[row]
        conv_b = jnp.sum(sb_ref[i] * cbw_ref[0:K_B - 1, :], axis=0, keepdims=True)
        conv_b = conv_b + cbw_ref[K_B - 1:K_B, :] * ub[row] + cbb_ref[...]
        preb_ref[row, :] = _layernorm_swish(conv_b, lng_ref[...], lnb_ref[...])
        nb_ref[i, 0:K_B - 2, :] = sb_ref[i, 1:K_B - 1, :]
        nb_ref[i, K_B - 2:K_B - 1, :] = ub[row]
        prod = (k_ref[i] * q[row]).astype(BF16)
        s = jnp.concatenate(
            [_dot(prod[:, hd * dh:(hd + 1) * dh], ones) for hd in range(N_HEADS)], axis=-1)
        e = jnp.exp(s - jnp.max(s, axis=0, keepdims=True))
        den = jnp.sum(e, axis=0, keepdims=True)
        o_ref[row, :] = jnp.sum(e * v_ref[i], axis=0, keepdims=True) * (1.0 / den)


def _stoken_call(proj, sa, sb, k, v, caw, cbw, cbb, lng, lnb):
    m, n_mem, d = k.shape
    tb = SAMPLE_TOKENS
    kernel = functools.partial(_stoken_kernel, tb=tb, d=d)
    row_spec = pl.BlockSpec((tb, d), lambda i: (i, 0))
    return pl.pallas_call(
        kernel,
        grid=(m // tb,),
        in_specs=[
            pl.BlockSpec((tb, 6 * d), lambda i: (i, 0)),
            pl.BlockSpec((tb, K_A - 1, d), lambda i: (i, 0, 0)),
            pl.BlockSpec((tb, K_B - 1, d), lambda i: (i, 0, 0)),
            pl.BlockSpec((tb, n_mem, d), lambda i: (i, 0, 0)),
            pl.BlockSpec((tb, n_mem, d), lambda i: (i, 0, 0)),
            _resident((K_A, d)),
            _resident((K_B, d)),
            _resident((1, d)), _resident((1, d)), _resident((1, d)),
        ],
        out_specs=[
            row_spec, row_spec, row_spec,
            pl.BlockSpec((tb, K_A - 1, d), lambda i: (i, 0, 0)),
            pl.BlockSpec((tb, K_B - 1, d), lambda i: (i, 0, 0)),
        ],
        out_shape=[
            jax.ShapeDtypeStruct((m, d), F32),
            jax.ShapeDtypeStruct((m, d), F32),
            jax.ShapeDtypeStruct((m, d), F32),
            jax.ShapeDtypeStruct((m, K_A - 1, d), F32),
            jax.ShapeDtypeStruct((m, K_B - 1, d), F32),
        ],
        compiler_params=pltpu.CompilerParams(
            dimension_semantics=("arbitrary",), vmem_limit_bytes=48 * MIB),
        name="sample_token",
    )(proj, sa, sb, k, v, caw, cbw, cbb, lng, lnb)


def _spost_kernel(x_ref, prea_ref, preb_ref, o_ref, gl_ref, bgate_ref,
                  woa_ref, wob_ref, wox_ref, wo_ref, h_ref, *, d):
    ya = _dot(prea_ref[...].astype(BF16), woa_ref[...])
    yb = _dot(preb_ref[...].astype(BF16), wob_ref[...])
    yx = _dot(o_ref[...].astype(BF16), wox_ref[...])
    g = _sigmoid(gl_ref[...] + bgate_ref[...])
    merged = g[:, :d] * ya + g[:, d:2 * d] * yb + g[:, 2 * d:] * yx
    h_ref[...] = x_ref[...] + _dot(merged.astype(BF16), wo_ref[...])


def _spost_call(x, prea, preb, o, proj, bgate, woa, wob, wox, wo):
    m, d = x.shape
    kernel = functools.partial(_spost_kernel, d=d)
    full = pl.BlockSpec((m, d), lambda i: (0, 0))
    return pl.pallas_call(
        kernel,
        grid=(1,),
        in_specs=[
            full, full, full, full,
            pl.BlockSpec((m, 3 * d), lambda i: (0, 2)),
            _resident((1, 3 * d)),
            _resident((d, d)), _resident((d, d)), _resident((d, d)), _resident((d, d)),
        ],
        out_specs=full,
        out_shape=jax.ShapeDtypeStruct((m, d), F32),
        compiler_params=pltpu.CompilerParams(
            dimension_semantics=("arbitrary",), vmem_limit_bytes=32 * MIB),
        name="sample_post",
    )(x, prea, preb, o, proj, bgate, woa, wob, wox, wo)


def kernel(x_prompt, x_sample, mem_prompt, cache_mem_k, cache_mem_v, state_conv_a, state_conv_b, norm_mix, w_in, b_gate, conv_a_w, w_out_a, conv_b_w, conv_b_bias, ln_b_g, ln_b_b, w_out_b, norm_mem, w_k, w_v, w_out_x, w_o, norm_ffn, w_ff_gate, w_ff_up, w_ff_down, norm_final):
    depth = w_in.shape[0]
    assert depth == 1, "single-layer step only"
    b, t, d = x_prompt.shape
    sb_, st_, _ = x_sample.shape
    assert st_ == 1
    n_mem = mem_prompt.shape[1]
    nh, dh = w_k.shape[2], w_k.shape[3]
    assert nh == N_HEADS and nh * dh == d

    bf = lambda w: w.astype(BF16)
    row = lambda v: v.reshape(1, -1)
    w_in_b = bf(w_in[0])
    woa, wob, wox, wo = bf(w_out_a[0]), bf(w_out_b[0]), bf(w_out_x[0]), bf(w_o[0])
    wk = bf(w_k[0].reshape(d, d))
    wv = bf(w_v[0].reshape(d, d))
    wkt = wk.T
    wg, wu, wd = bf(w_ff_gate[0]), bf(w_ff_up[0]), bf(w_ff_down[0])
    nmix, nffn, nfin, nmem = row(norm_mix[0]), row(norm_ffn[0]), row(norm_final), row(norm_mem[0])
    bgate, cbb, lng, lnb = row(b_gate[0]), row(conv_b_bias[0]), row(ln_b_g[0]), row(ln_b_b[0])
    caw, cbw = conv_a_w[0], conv_b_w[0]

    k_p, v_p, kt, vb = _kv_call(mem_prompt, nmem, wk, wv, wkt)
    h_p, ca_p, cb_p = _mixer_call(x_prompt, kt, vb, w_in_b, woa, wob, wox, wo,
                                  nmix, bgate, caw, cbw, cbb, lng, lnb)
    y_p = _ffn_call(h_p.reshape(b * t, d), nffn, wg, wu, wd, nfin, PROMPT_TILE)

    xs = x_sample.reshape(sb_, d)
    proj_s = _sproj_call(xs, nmix, w_in_b)
    prea, preb, o_s, ca_s, cb_s = _stoken_call(
        proj_s, state_conv_a[0], state_conv_b[0],
        cache_mem_k[0].reshape(sb_, n_mem, d), cache_mem_v[0].reshape(sb_, n_mem, d),
        caw, cbw, cbb, lng, lnb)
    h_s = _spost_call(xs, prea, preb, o_s, proj_s, bgate, woa, wob, wox, wo)
    y_s = _ffn_call(h_s, nffn, wg, wu, wd, nfin, sb_)

    return (y_p.reshape(b, t, d), y_s.reshape(sb_, 1, d),
            k_p.reshape(1, b, n_mem, nh, dh), v_p.reshape(1, b, n_mem, nh, dh),
            ca_p[None], cb_p[None], ca_s[None], cb_s[None])
```

```python
import functools

import jax
import jax.numpy as jnp
from jax import lax
from jax.experimental import pallas as pl
from jax.experimental.pallas import tpu as pltpu

F32 = jnp.float32
BF16 = jnp.bfloat16
EPS = 1e-6

K_A = 3
K_B = 31
N_HEADS = 4
SUBLANES = 8
LANES = 128
HALO_A = 8
HALO_B = 32
CONV_ROWS = 32

PROMPT_TILE = 512
SAMPLE_TOKENS = 8
MIB = 1024 * 1024


def _dot(a, b):
    return jnp.dot(a, b, preferred_element_type=F32)


def _rms(x, g):
    ms = jnp.mean(x * x, axis=-1, keepdims=True)
    return x * lax.rsqrt(ms + EPS) * g


def _sigmoid(x):
    return 1.0 / (1.0 + jnp.exp(-x))


def _layernorm_swish(y, g, b):
    mu = jnp.mean(y, axis=-1, keepdims=True)
    yc = y - mu
    var = jnp.mean(yc * yc, axis=-1, keepdims=True)
    z = yc * lax.rsqrt(var + EPS) * g + b
    return z * _sigmoid(z)


def _resident(shape):
    nd = len(shape)
    return pl.BlockSpec(shape, lambda *_: (0,) * nd, pipeline_mode=pl.Buffered(1))


def _kv_kernel(mem_ref, g_ref, wk_ref, wv_ref, wkt_ref, k_ref, v_ref, kt_ref, vb_ref):
    mn = _rms(mem_ref[0], g_ref[...]).astype(BF16)
    k_ref[0] = _dot(mn, wk_ref[...])
    v = _dot(mn, wv_ref[...])
    v_ref[0] = v
    vb_ref[0] = v.astype(BF16)
    kt = lax.dot_general(wkt_ref[...], mn, (((1,), (1,)), ((), ())),
                         preferred_element_type=F32)
    kt_ref[0] = kt.astype(BF16)


def _kv_call(mem, g, wk, wv, wkt):
    b, n_mem, d = mem.shape
    dx = wk.shape[1]
    return pl.pallas_call(
        _kv_kernel,
        grid=(b,),
        in_specs=[
            pl.BlockSpec((1, n_mem, d), lambda i: (i, 0, 0)),
            _resident((1, d)),
            _resident((d, dx)),
            _resident((d, dx)),
            _resident((dx, d)),
        ],
        out_specs=[
            pl.BlockSpec((1, n_mem, dx), lambda i: (i, 0, 0)),
            pl.BlockSpec((1, n_mem, dx), lambda i: (i, 0, 0)),
            pl.BlockSpec((1, dx, n_mem), lambda i: (i, 0, 0)),
            pl.BlockSpec((1, n_mem, dx), lambda i: (i, 0, 0)),
        ],
        out_shape=[
            jax.ShapeDtypeStruct((b, n_mem, dx), F32),
            jax.ShapeDtypeStruct((b, n_mem, dx), F32),
            jax.ShapeDtypeStruct((b, dx, n_mem), BF16),
            jax.ShapeDtypeStruct((b, n_mem, dx), BF16),
        ],
        compiler_params=pltpu.CompilerParams(
            dimension_semantics=("arbitrary",), vmem_limit_bytes=32 * MIB),
        name="kv",
    )(mem, g, wk, wv, wkt)


def _mixer_kernel(x_ref, kt_ref, vb_ref, w_in_ref, woa_ref, wob_ref, wox_ref, wo_ref,
                  nmix_ref, bgate_ref, caw_ref, cbw_ref, cbb_ref, lng_ref, lnb_ref,
                  h_ref, na_ref, nb_ref, ua_buf, ub_buf, preb_buf, *, tm, d):
    t = pl.program_id(1)
    dh = d // N_HEADS

    @pl.when(t == 0)
    def _():
        ua_buf[0:HALO_A, :] = jnp.zeros((HALO_A, d), F32)
        ub_buf[0:HALO_B, :] = jnp.zeros((HALO_B, d), F32)

    x = x_ref[0]
    xn = _rms(x, nmix_ref[...]).astype(BF16)

    def proj(c0, c1):
        return _dot(xn, w_in_ref[:, c0 * d:c1 * d])

    cx = proj(1, 3)
    ua_buf[HALO_A:HALO_A + tm, :] = cx[:, :d] * cx[:, d:]
    conv_a = caw_ref[0:1, :] * ua_buf[HALO_A - 2:HALO_A - 2 + tm, :]
    conv_a = conv_a + caw_ref[1:2, :] * ua_buf[HALO_A - 1:HALO_A - 1 + tm, :]
    conv_a = conv_a + caw_ref[2:3, :] * ua_buf[HALO_A:HALO_A + tm, :]
    ya = _dot((proj(0, 1) * conv_a).astype(BF16), woa_ref[...])
    na_ref[0] = ua_buf[HALO_A + tm - (K_A - 1):HALO_A + tm, :]
    ua_buf[0:HALO_A, :] = ua_buf[tm:tm + HALO_A, :]

    vg = proj(3, 5)
    ub_buf[HALO_B:HALO_B + tm, :] = vg[:, :d] * _sigmoid(vg[:, d:])

    def conv_rows(i, carry):
        r0 = pl.multiple_of(i * CONV_ROWS, CONV_ROWS)
        win_rows = CONV_ROWS + HALO_B
        blocks = []
        for lb in range(d // LANES):
            lanes = slice(lb * LANES, (lb + 1) * LANES)
            win = ub_buf[pl.ds(r0, win_rows), lanes]
            acc = jnp.zeros((CONV_ROWS, LANES), F32)
            for r in range(SUBLANES):
                rolled = win if r == 0 else pltpu.roll(win, win_rows - r, 0)
                for a in range(win_rows // SUBLANES):
                    j = SUBLANES * a + r - (HALO_B - (K_B - 1))
                    if 0 <= j < K_B:
                        rows = rolled[SUBLANES * a:SUBLANES * a + CONV_ROWS, :]
                        acc = acc + cbw_ref[j:j + 1, lanes] * rows
            blocks.append(acc)
        acc = jnp.concatenate(blocks, axis=-1)
        z = _layernorm_swish(acc + cbb_ref[...], lng_ref[...], lnb_ref[...])
        preb_buf[pl.ds(r0, CONV_ROWS), :] = z.astype(BF16)
        return carry

    lax.fori_loop(0, tm // CONV_ROWS, conv_rows, 0)
    yb = _dot(preb_buf[...], wob_ref[...])
    nb_ref[0] = ub_buf[HALO_B + tm - (K_B - 1):HALO_B + tm, :]
    ub_buf[0:HALO_B, :] = ub_buf[tm:tm + HALO_B, :]

    q = (proj(5, 6) * (dh ** -0.5)).astype(BF16)
    heads = []
    for hd in range(N_HEADS):
        sl = slice(hd * dh, (hd + 1) * dh)
        s = _dot(q[:, sl], kt_ref[0, sl, :])
        e = jnp.exp(s - jnp.max(s, axis=-1, keepdims=True))
        p = e * (1.0 / jnp.sum(e, axis=-1, keepdims=True))
        heads.append(_dot(p.astype(BF16), vb_ref[0, :, sl]))
    yx = _dot(jnp.concatenate(heads, axis=-1).astype(BF16), wox_ref[...])

    g = _sigmoid(proj(6, 9) + bgate_ref[...])
    merged = g[:, :d] * ya + g[:, d:2 * d] * yb + g[:, 2 * d:] * yx
    h_ref[0] = x + _dot(merged.astype(BF16), wo_ref[...])


def _mixer_call(x, kt, vb, w_in, woa, wob, wox, wo, nmix, bgate, caw, cbw, cbb, lng, lnb):
    b, t, d = x.shape
    tm = PROMPT_TILE
    n_mem = vb.shape[1]
    d_in = w_in.shape[1]
    kernel = functools.partial(_mixer_kernel, tm=tm, d=d)
    return pl.pallas_call(
        kernel,
        grid=(b, t // tm),
        in_specs=[
            pl.BlockSpec((1, tm, d), lambda i, j: (i, j, 0)),
            pl.BlockSpec((1, d, n_mem), lambda i, j: (i, 0, 0)),
            pl.BlockSpec((1, n_mem, d), lambda i, j: (i, 0, 0)),
            _resident((d, d_in)),
            _resident((d, d)), _resident((d, d)), _resident((d, d)), _resident((d, d)),
            _resident((1, d)),
            _resident((1, 3 * d)),
            _resident((K_A, d)),
            _resident((K_B, d)),
            _resident((1, d)), _resident((1, d)), _resident((1, d)),
        ],
        out_specs=[
            pl.BlockSpec((1, tm, d), lambda i, j: (i, j, 0)),
            pl.BlockSpec((1, K_A - 1, d), lambda i, j: (i, 0, 0)),
            pl.BlockSpec((1, K_B - 1, d), lambda i, j: (i, 0, 0)),
        ],
        out_shape=[
            jax.ShapeDtypeStruct((b, t, d), F32),
            jax.ShapeDtypeStruct((b, K_A - 1, d), F32),
            jax.ShapeDtypeStruct((b, K_B - 1, d), F32),
        ],
        scratch_shapes=[
            pltpu.VMEM((HALO_A + tm, d), F32),
            pltpu.VMEM((HALO_B + tm, d), F32),
            pltpu.VMEM((tm, d), BF16),
        ],
        compiler_params=pltpu.CompilerParams(
            dimension_semantics=("arbitrary", "arbitrary"), vmem_limit_bytes=58 * MIB),
        name="mixer",
    )(x, kt, vb, w_in, woa, wob, wox, wo, nmix, bgate, caw, cbw, cbb, lng, lnb)


def _ffn_kernel(h_ref, nffn_ref, wg_ref, wu_ref, wd_ref, nfin_ref, y_ref):
    h = h_ref[...]
    hn = _rms(h, nffn_ref[...]).astype(BF16)
    gate = _dot(hn, wg_ref[...])
    up = _dot(hn, wu_ref[...])
    act = (gate * _sigmoid(gate) * up).astype(BF16)
    h2 = h + _dot(act, wd_ref[...])
    y_ref[...] = _rms(h2, nfin_ref[...])


def _ffn_call(h, nffn, wg, wu, wd, nfin, tm):
    m, d = h.shape
    dff = wg.shape[1]
    return pl.pallas_call(
        _ffn_kernel,
        grid=(m // tm,),
        in_specs=[
            pl.BlockSpec((tm, d), lambda i: (i, 0)),
            _resident((1, d)),
            _resident((d, dff)), _resident((d, dff)), _resident((dff, d)),
            _resident((1, d)),
        ],
        out_specs=pl.BlockSpec((tm, d), lambda i: (i, 0)),
        out_shape=jax.ShapeDtypeStruct((m, d), F32),
        compiler_params=pltpu.CompilerParams(
            dimension_semantics=("arbitrary",), vmem_limit_bytes=56 * MIB),
        name="ffn",
    )(h, nffn, wg, wu, wd, nfin)


def _sproj_kernel(x_ref, nmix_ref, w_ref, o_ref):
    xn = _rms(x_ref[...], nmix_ref[...]).astype(BF16)
    o_ref[...] = _dot(xn, w_ref[...])


def _sproj_call(x, nmix, w_in):
    m, d = x.shape
    d_in = w_in.shape[1]
    return pl.pallas_call(
        _sproj_kernel,
        grid=(d_in // d,),
        in_specs=[
            _resident((m, d)),
            _resident((1, d)),
            pl.BlockSpec((d, d), lambda j: (0, j)),
        ],
        out_specs=pl.BlockSpec((m, d), lambda j: (0, j)),
        out_shape=jax.ShapeDtypeStruct((m, d_in), F32),
        compiler_params=pltpu.CompilerParams(
            dimension_semantics=("arbitrary",), vmem_limit_bytes=32 * MIB),
        name="sample_proj",
    )(x, nmix, w_in)


def _stoken_kernel(pr_ref, qv_ref, sa_ref, sb_ref, k_ref, v_ref, caw_ref, cbw_ref, cbb_ref,
                   lng_ref, lnb_ref, prea_ref, preb_ref, o_ref, na_ref, nb_ref, *, tb, d):
    a_b = pr_ref[:, 0:d]
    ua = pr_ref[:, d:2 * d] * pr_ref[:, 2 * d:3 * d]
    ub = pr_ref[:, 3 * d:4 * d] * _sigmoid(pr_ref[:, 4 * d:5 * d])

    conv_b = cbw_ref[K_B - 1:K_B, :] * ub + cbb_ref[...]
    for j in range(K_B - 1):
        conv_b = conv_b + cbw_ref[j:j + 1, :] * sb_ref[j]
    preb_ref[...] = _layernorm_swish(conv_b, lng_ref[...], lnb_ref[...])
    for j in range(K_B - 2):
        nb_ref[j] = sb_ref[j + 1]
    nb_ref[K_B - 2] = ub

    n_mem, cs, lanes = k_ref.shape[1:]
    ones = jnp.ones((lanes, lanes), BF16)
    scale = (cs // N_HEADS * lanes) ** -0.5
    for i in range(tb):
        row = slice(i, i + 1)
        conv_a = (caw_ref[0:1, :] * sa_ref[i, 0:1, :] + caw_ref[1:2, :] * sa_ref[i, 1:2, :]
                  + caw_ref[2:3, :] * ua[row])
        prea_ref[row, :] = a_b[row] * conv_a
        na_ref[i, 0:1, :] = sa_ref[i, 1:2, :]
        na_ref[i, 1:2, :] = ua[row]
        prod = k_ref[i] * (qv_ref[i] * scale)[None]
        prod = prod + pltpu.roll(prod, N_HEADS, 1)
        s = _dot(prod.reshape(n_mem * cs, lanes).astype(BF16), ones).reshape(n_mem, cs, lanes)
        e = jnp.exp(s - jnp.max(s, axis=0, keepdims=True))
        den = jnp.sum(e, axis=0)
        o_ref[i] = jnp.sum(e * v_ref[i], axis=0) * (1.0 / den)


def _stoken_call(proj, qv, sa, sb, k, v, caw, cbw, cbb, lng, lnb):
    m, n_mem, cs, lanes = k.shape
    d = cs * lanes
    tb = SAMPLE_TOKENS
    kernel = functools.partial(_stoken_kernel, tb=tb, d=d)
    row_spec = pl.BlockSpec((tb, d), lambda i: (i, 0))
    chunk_spec = pl.BlockSpec((tb, cs, lanes), lambda i: (i, 0, 0))
    cache_spec = pl.BlockSpec((tb, n_mem, cs, lanes), lambda i: (i, 0, 0, 0))
    return pl.pallas_call(
        kernel,
        grid=(m // tb,),
        in_specs=[
            pl.BlockSpec((tb, 5 * d), lambda i: (i, 0)),
            chunk_spec,
            pl.BlockSpec((tb, K_A - 1, d), lambda i: (i, 0, 0)),
            pl.BlockSpec((K_B - 1, tb, d), lambda i: (0, i, 0)),
            cache_spec, cache_spec,
            _resident((K_A, d)),
            _resident((K_B, d)),
            _resident((1, d)), _resident((1, d)), _resident((1, d)),
        ],
        out_specs=[
            row_spec, row_spec, chunk_spec,
            pl.BlockSpec((tb, K_A - 1, d), lambda i: (i, 0, 0)),
            pl.BlockSpec((K_B - 1, tb, d), lambda i: (0, i, 0)),
        ],
        out_shape=[
            jax.ShapeDtypeStruct((m, d), F32),
            jax.ShapeDtypeStruct((m, d), F32),
            jax.ShapeDtypeStruct((m, cs, lanes), F32),
            jax.ShapeDtypeStruct((m, K_A - 1, d), F32),
            jax.ShapeDtypeStruct((K_B - 1, m, d), F32),
        ],
        compiler_params=pltpu.CompilerParams(
            dimension_semantics=("arbitrary",), vmem_limit_bytes=48 * MIB),
        name="sample_token",
    )(proj, qv, sa, sb, k, v, caw, cbw, cbb, lng, lnb)


def _to_chunks(a, lanes):
    *lead, nh, dh = a.shape
    a = a.reshape(*lead, nh, dh // lanes, lanes)
    a = jnp.swapaxes(a, -3, -2)
    return a.reshape(*lead, (dh // lanes) * nh, lanes)


def _from_chunks(a, nh):
    *lead, cs, lanes = a.shape
    a = a.reshape(*lead, cs // nh, nh, lanes)
    a = jnp.swapaxes(a, -3, -2)
    return a.reshape(*lead, cs * lanes)


def _spost_kernel(x_ref, prea_ref, preb_ref, o_ref, gl_ref, bgate_ref,
                  woa_ref, wob_ref, wox_ref, wo_ref, h_ref, *, d):
    ya = _dot(prea_ref[...].astype(BF16), woa_ref[...])
    yb = _dot(preb_ref[...].astype(BF16), wob_ref[...])
    yx = _dot(o_ref[...].astype(BF16), wox_ref[...])
    g = _sigmoid(gl_ref[...] + bgate_ref[...])
    merged = g[:, :d] * ya + g[:, d:2 * d] * yb + g[:, 2 * d:] * yx
    h_ref[...] = x_ref[...] + _dot(merged.astype(BF16), wo_ref[...])


def _spost_call(x, prea, preb, o, proj, bgate, woa, wob, wox, wo):
    m, d = x.shape
    kernel = functools.partial(_spost_kernel, d=d)
    full = pl.BlockSpec((m, d), lambda i: (0, 0))
    return pl.pallas_call(
        kernel,
        grid=(1,),
        in_specs=[
            full, full, full, full,
            pl.BlockSpec((m, 3 * d), lambda i: (0, 2)),
            _resident((1, 3 * d)),
            _resident((d, d)), _resident((d, d)), _resident((d, d)), _resident((d, d)),
        ],
        out_specs=full,
        out_shape=jax.ShapeDtypeStruct((m, d), F32),
        compiler_params=pltpu.CompilerParams(
            dimension_semantics=("arbitrary",), vmem_limit_bytes=32 * MIB),
        name="sample_post",
    )(x, prea, preb, o, proj, bgate, woa, wob, wox, wo)


def kernel(x_prompt, x_sample, mem_prompt, cache_mem_k, cache_mem_v, state_conv_a, state_conv_b, norm_mix, w_in, b_gate, conv_a_w, w_out_a, conv_b_w, conv_b_bias, ln_b_g, ln_b_b, w_out_b, norm_mem, w_k, w_v, w_out_x, w_o, norm_ffn, w_ff_gate, w_ff_up, w_ff_down, norm_final):
    depth = w_in.shape[0]
    assert depth == 1, "single-layer step only"
    b, t, d = x_prompt.shape
    sb_, st_, _ = x_sample.shape
    assert st_ == 1
    n_mem = mem_prompt.shape[1]
    nh, dh = w_k.shape[2], w_k.shape[3]
    assert nh == N_HEADS and nh * dh == d

    bf = lambda w: w.astype(BF16)
    row = lambda v: v.reshape(1, -1)
    w_in_b = bf(w_in[0])
    woa, wob, wox, wo = bf(w_out_a[0]), bf(w_out_b[0]), bf(w_out_x[0]), bf(w_o[0])
    wk = bf(w_k[0].reshape(d, d))
    wv = bf(w_v[0].reshape(d, d))
    wkt = wk.T
    wg, wu, wd = bf(w_ff_gate[0]), bf(w_ff_up[0]), bf(w_ff_down[0])
    nmix, nffn, nfin, nmem = row(norm_mix[0]), row(norm_ffn[0]), row(norm_final), row(norm_mem[0])
    bgate, cbb, lng, lnb = row(b_gate[0]), row(conv_b_bias[0]), row(ln_b_g[0]), row(ln_b_b[0])
    caw, cbw = conv_a_w[0], conv_b_w[0]

    k_p, v_p, kt, vb = _kv_call(mem_prompt, nmem, wk, wv, wkt)
    h_p, ca_p, cb_p = _mixer_call(x_prompt, kt, vb, w_in_b, woa, wob, wox, wo,
                                  nmix, bgate, caw, cbw, cbb, lng, lnb)
    y_p = _ffn_call(h_p.reshape(b * t, d), nffn, wg, wu, wd, nfin, PROMPT_TILE)

    xs = x_sample.reshape(sb_, d)
    proj_s = _sproj_call(xs, nmix, w_in_b)
    assert dh == 2 * LANES and 2 * nh == SUBLANES
    q_s = proj_s[:, 5 * d:6 * d].reshape(sb_, nh, dh)
    prea, preb, o_s, ca_s, cb_s = _stoken_call(
        proj_s, _to_chunks(q_s, LANES), state_conv_a[0],
        jnp.swapaxes(state_conv_b[0], 0, 1),
        _to_chunks(cache_mem_k[0], LANES), _to_chunks(cache_mem_v[0], LANES),
        caw, cbw, cbb, lng, lnb)
    h_s = _spost_call(xs, prea, preb, _from_chunks(o_s, nh), proj_s, bgate, woa, wob, wox, wo)
    y_s = _ffn_call(h_s, nffn, wg, wu, wd, nfin, sb_)

    return (y_p.reshape(b, t, d), y_s.reshape(sb_, 1, d),
            k_p.reshape(1, b, n_mem, nh, dh), v_p.reshape(1, b, n_mem, nh, dh),
            ca_p[None], cb_p[None], ca_s[None], jnp.swapaxes(cb_s, 0, 1)[None])
```

```python
import functools

import jax
import jax.numpy as jnp
from jax import lax
from jax.experimental import pallas as pl
from jax.experimental.pallas import tpu as pltpu

F32 = jnp.float32
BF16 = jnp.bfloat16
EPS = 1e-6

K_A = 3
K_B = 31
N_HEADS = 4
SUBLANES = 8
LANES = 128
HALO_A = 8
HALO_B = 32
CONV_ROWS = 64

PROMPT_TILE = 512
SAMPLE_TOKENS = 8
MIB = 1024 * 1024


def _dot(a, b):
    return jnp.dot(a, b, preferred_element_type=F32)


def _rms(x, g):
    ms = jnp.mean(x * x, axis=-1, keepdims=True)
    return x * lax.rsqrt(ms + EPS) * g


def _sigmoid(x):
    return 1.0 / (1.0 + jnp.exp(-x))


def _layernorm_swish(y, g, b):
    mu = jnp.mean(y, axis=-1, keepdims=True)
    yc = y - mu
    var = jnp.mean(yc * yc, axis=-1, keepdims=True)
    z = yc * lax.rsqrt(var + EPS) * g + b
    return z * _sigmoid(z)


def _resident(shape):
    nd = len(shape)
    return pl.BlockSpec(shape, lambda *_: (0,) * nd, pipeline_mode=pl.Buffered(1))


def _kv_kernel(mem_ref, g_ref, wk_ref, wv_ref, wkt_ref, k_ref, v_ref, kt_ref, vb_ref):
    mn = _rms(mem_ref[0], g_ref[...]).astype(BF16)
    k_ref[0] = _dot(mn, wk_ref[...])
    v = _dot(mn, wv_ref[...])
    v_ref[0] = v
    vb_ref[0] = v.astype(BF16)
    kt = lax.dot_general(wkt_ref[...], mn, (((1,), (1,)), ((), ())),
                         preferred_element_type=F32)
    kt_ref[0] = kt.astype(BF16)


def _kv_call(mem, g, wk, wv, wkt):
    b, n_mem, d = mem.shape
    dx = wk.shape[1]
    return pl.pallas_call(
        _kv_kernel,
        grid=(b,),
        in_specs=[
            pl.BlockSpec((1, n_mem, d), lambda i: (i, 0, 0)),
            _resident((1, d)),
            _resident((d, dx)),
            _resident((d, dx)),
            _resident((dx, d)),
        ],
        out_specs=[
            pl.BlockSpec((1, n_mem, dx), lambda i: (i, 0, 0)),
            pl.BlockSpec((1, n_mem, dx), lambda i: (i, 0, 0)),
            pl.BlockSpec((1, dx, n_mem), lambda i: (i, 0, 0)),
            pl.BlockSpec((1, n_mem, dx), lambda i: (i, 0, 0)),
        ],
        out_shape=[
            jax.ShapeDtypeStruct((b, n_mem, dx), F32),
            jax.ShapeDtypeStruct((b, n_mem, dx), F32),
            jax.ShapeDtypeStruct((b, dx, n_mem), BF16),
            jax.ShapeDtypeStruct((b, n_mem, dx), BF16),
        ],
        compiler_params=pltpu.CompilerParams(
            dimension_semantics=("arbitrary",), vmem_limit_bytes=32 * MIB),
        name="kv",
    )(mem, g, wk, wv, wkt)


def _mixer_kernel(x_ref, kt_ref, vb_ref, w_in_ref, woa_ref, wob_ref, wox_ref, wo_ref,
                  nmix_ref, bgate_ref, caw_ref, cbw_ref, cbb_ref, lng_ref, lnb_ref,
                  h_ref, na_ref, nb_ref, ua_buf, ub_buf, convb_buf, preb_buf, *, tm, d):
    t = pl.program_id(1)
    dh = d // N_HEADS

    @pl.when(t == 0)
    def _():
        ua_buf[0:HALO_A, :] = jnp.zeros((HALO_A, d), F32)
        ub_buf[0:HALO_B, :] = jnp.zeros((HALO_B, d), F32)

    x = x_ref[0]
    xn = _rms(x, nmix_ref[...]).astype(BF16)

    def proj(c0, c1):
        return _dot(xn, w_in_ref[:, c0 * d:c1 * d])

    def gate(i):
        return _sigmoid(proj(6 + i, 7 + i) + bgate_ref[:, i * d:(i + 1) * d])

    vg = proj(3, 5)
    ub_buf[HALO_B:HALO_B + tm, :] = vg[:, :d] * _sigmoid(vg[:, d:])

    def conv_chunk(c):
        r0 = c * CONV_ROWS
        win_rows = CONV_ROWS + HALO_B
        for lb in range(d // LANES):
            lanes = slice(lb * LANES, (lb + 1) * LANES)
            win = ub_buf[r0:r0 + win_rows, lanes]
            acc = jnp.zeros((CONV_ROWS, LANES), F32)
            for r in range(SUBLANES):
                rolled = win if r == 0 else pltpu.roll(win, win_rows - r, 0)
                for a in range(win_rows // SUBLANES):
                    j = SUBLANES * a + r - (HALO_B - (K_B - 1))
                    if 0 <= j < K_B:
                        rows = rolled[SUBLANES * a:SUBLANES * a + CONV_ROWS, :]
                        acc = acc + cbw_ref[j:j + 1, lanes] * rows
            convb_buf[r0:r0 + CONV_ROWS, lanes] = acc + cbb_ref[:, lanes]
        z = _layernorm_swish(convb_buf[r0:r0 + CONV_ROWS, :], lng_ref[...], lnb_ref[...])
        preb_buf[r0:r0 + CONV_ROWS, :] = z.astype(BF16)

    out = {}

    def branch_a():
        cx = proj(1, 3)
        ua_buf[HALO_A:HALO_A + tm, :] = cx[:, :d] * cx[:, d:]
        conv_a = caw_ref[0:1, :] * ua_buf[HALO_A - 2:HALO_A - 2 + tm, :]
        conv_a = conv_a + caw_ref[1:2, :] * ua_buf[HALO_A - 1:HALO_A - 1 + tm, :]
        conv_a = conv_a + caw_ref[2:3, :] * ua_buf[HALO_A:HALO_A + tm, :]
        out["pre_a"] = (proj(0, 1) * conv_a).astype(BF16)
        na_ref[0] = ua_buf[HALO_A + tm - (K_A - 1):HALO_A + tm, :]
        ua_buf[0:HALO_A, :] = ua_buf[tm:tm + HALO_A, :]

    def branch_a_out():
        out["ya"] = gate(0) * _dot(out["pre_a"], woa_ref[...])

    def query():
        out["q"] = (proj(5, 6) * (dh ** -0.5)).astype(BF16)
        out["heads"] = []

    def head(hd):
        sl = slice(hd * dh, (hd + 1) * dh)
        s = _dot(out["q"][:, sl], kt_ref[0, sl, :])
        e = jnp.exp(s - jnp.max(s, axis=-1, keepdims=True))
        p = e * (1.0 / jnp.sum(e, axis=-1, keepdims=True))
        out["heads"].append(_dot(p.astype(BF16), vb_ref[0, :, sl]))

    def branch_x_out():
        o = jnp.concatenate(out["heads"], axis=-1).astype(BF16)
        out["yx"] = gate(2) * _dot(o, wox_ref[...])

    def gate_b():
        out["gb"] = gate(1)

    mxu_work = [branch_a, branch_a_out, query,
                functools.partial(head, 0), functools.partial(head, 1),
                functools.partial(head, 2), functools.partial(head, 3),
                branch_x_out, gate_b]
    n_chunks = tm // CONV_ROWS
    for c in range(n_chunks):
        conv_chunk(c)
        for work in mxu_work[c * len(mxu_work) // n_chunks:(c + 1) * len(mxu_work) // n_chunks]:
            work()

    nb_ref[0] = ub_buf[HALO_B + tm - (K_B - 1):HALO_B + tm, :]
    ub_buf[0:HALO_B, :] = ub_buf[tm:tm + HALO_B, :]

    yb = _dot(preb_buf[...], wob_ref[...])
    merged = out["ya"] + out["gb"] * yb + out["yx"]
    h_ref[0] = x + _dot(merged.astype(BF16), wo_ref[...])


def _mixer_call(x, kt, vb, w_in, woa, wob, wox, wo, nmix, bgate, caw, cbw, cbb, lng, lnb):
    b, t, d = x.shape
    tm = PROMPT_TILE
    n_mem = vb.shape[1]
    d_in = w_in.shape[1]
    kernel = functools.partial(_mixer_kernel, tm=tm, d=d)
    return pl.pallas_call(
        kernel,
        grid=(b, t // tm),
        in_specs=[
            pl.BlockSpec((1, tm, d), lambda i, j: (i, j, 0)),
            pl.BlockSpec((1, d, n_mem), lambda i, j: (i, 0, 0)),
            pl.BlockSpec((1, n_mem, d), lambda i, j: (i, 0, 0)),
            _resident((d, d_in)),
            _resident((d, d)), _resident((d, d)), _resident((d, d)), _resident((d, d)),
            _resident((1, d)),
            _resident((1, 3 * d)),
            _resident((K_A, d)),
            _resident((K_B, d)),
            _resident((1, d)), _resident((1, d)), _resident((1, d)),
        ],
        out_specs=[
            pl.BlockSpec((1, tm, d), lambda i, j: (i, j, 0)),
            pl.BlockSpec((1, K_A - 1, d), lambda i, j: (i, 0, 0)),
            pl.BlockSpec((1, K_B - 1, d), lambda i, j: (i, 0, 0)),
        ],
        out_shape=[
            jax.ShapeDtypeStruct((b, t, d), F32),
            jax.ShapeDtypeStruct((b, K_A - 1, d), F32),
            jax.ShapeDtypeStruct((b, K_B - 1, d), F32),
        ],
        scratch_shapes=[
            pltpu.VMEM((HALO_A + tm, d), F32),
            pltpu.VMEM((HALO_B + tm, d), F32),
            pltpu.VMEM((tm, d), F32),
            pltpu.VMEM((tm, d), BF16),
        ],
        compiler_params=pltpu.CompilerParams(
            dimension_semantics=("arbitrary", "arbitrary"), vmem_limit_bytes=62 * MIB),
        name="mixer",
    )(x, kt, vb, w_in, woa, wob, wox, wo, nmix, bgate, caw, cbw, cbb, lng, lnb)


def _ffn_kernel(h_ref, nffn_ref, wg_ref, wu_ref, wd_ref, nfin_ref, y_ref):
    h = h_ref[...]
    hn = _rms(h, nffn_ref[...]).astype(BF16)
    gate = _dot(hn, wg_ref[...])
    up = _dot(hn, wu_ref[...])
    act = (gate * _sigmoid(gate) * up).astype(BF16)
    h2 = h + _dot(act, wd_ref[...])
    y_ref[...] = _rms(h2, nfin_ref[...])


def _ffn_call(h, nffn, wg, wu, wd, nfin, tm):
    m, d = h.shape
    dff = wg.shape[1]
    return pl.pallas_call(
        _ffn_kernel,
        grid=(m // tm,),
        in_specs=[
            pl.BlockSpec((tm, d), lambda i: (i, 0)),
            _resident((1, d)),
            _resident((d, dff)), _resident((d, dff)), _resident((dff, d)),
            _resident((1, d)),
        ],
        out_specs=pl.BlockSpec((tm, d), lambda i: (i, 0)),
        out_shape=jax.ShapeDtypeStruct((m, d), F32),
        compiler_params=pltpu.CompilerParams(
            dimension_semantics=("arbitrary",), vmem_limit_bytes=56 * MIB),
        name="ffn",
    )(h, nffn, wg, wu, wd, nfin)


def _sproj_kernel(x_ref, nmix_ref, w_ref, o_ref):
    xn = _rms(x_ref[...], nmix_ref[...]).astype(BF16)
    o_ref[...] = _dot(xn, w_ref[...])


def _sproj_call(x, nmix, w_in):
    m, d = x.shape
    d_in = w_in.shape[1]
    return pl.pallas_call(
        _sproj_kernel,
        grid=(d_in // d,),
        in_specs=[
            _resident((m, d)),
            _resident((1, d)),
            pl.BlockSpec((d, d), lambda j: (0, j)),
        ],
        out_specs=pl.BlockSpec((m, d), lambda j: (0, j)),
        out_shape=jax.ShapeDtypeStruct((m, d_in), F32),
        compiler_params=pltpu.CompilerParams(
            dimension_semantics=("arbitrary",), vmem_limit_bytes=32 * MIB),
        name="sample_proj",
    )(x, nmix, w_in)


def _stoken_kernel(pr_ref, qv_ref, sa_ref, sb_ref, k_ref, v_ref, caw_ref, cbw_ref, cbb_ref,
                   lng_ref, lnb_ref, prea_ref, preb_ref, o_ref, na_ref, nb_ref, *, tb, d):
    a_b = pr_ref[:, 0:d]
    ua = pr_ref[:, d:2 * d] * pr_ref[:, 2 * d:3 * d]
    ub = pr_ref[:, 3 * d:4 * d] * _sigmoid(pr_ref[:, 4 * d:5 * d])

    conv_b = cbw_ref[K_B - 1:K_B, :] * ub + cbb_ref[...]
    for j in range(K_B - 1):
        conv_b = conv_b + cbw_ref[j:j + 1, :] * sb_ref[j]
    preb_ref[...] = _layernorm_swish(conv_b, lng_ref[...], lnb_ref[...])
    for j in range(K_B - 2):
        nb_ref[j] = sb_ref[j + 1]
    nb_ref[K_B - 2] = ub

    n_mem, cs, lanes = k_ref.shape[1:]
    ones = jnp.ones((lanes, lanes), BF16)
    scale = (cs // N_HEADS * lanes) ** -0.5
    for i in range(tb):
        row = slice(i, i + 1)
        conv_a = (caw_ref[0:1, :] * sa_ref[i, 0:1, :] + caw_ref[1:2, :] * sa_ref[i, 1:2, :]
                  + caw_ref[2:3, :] * ua[row])
        prea_ref[row, :] = a_b[row] * conv_a
        na_ref[i, 0:1, :] = sa_ref[i, 1:2, :]
        na_ref[i, 1:2, :] = ua[row]
        prod = k_ref[i] * (qv_ref[i] * scale)[None]
        prod = prod + pltpu.roll(prod, N_HEADS, 1)
        s = _dot(prod.reshape(n_mem * cs, lanes).astype(BF16), ones).reshape(n_mem, cs, lanes)
        e = jnp.exp(s - jnp.max(s, axis=0, keepdims=True))
        den = jnp.sum(e, axis=0)
        o_ref[i] = jnp.sum(e * v_ref[i], axis=0) * (1.0 / den)


def _stoken_call(proj, qv, sa, sb, k, v, caw, cbw, cbb, lng, lnb):
    m, n_mem, cs, lanes = k.shape
    d = cs * lanes
    tb = SAMPLE_TOKENS
    kernel = functools.partial(_stoken_kernel, tb=tb, d=d)
    row_spec = pl.BlockSpec((tb, d), lambda i: (i, 0))
    chunk_spec = pl.BlockSpec((tb, cs, lanes), lambda i: (i, 0, 0))
    cache_spec = pl.BlockSpec((tb, n_mem, cs, lanes), lambda i: (i, 0, 0, 0))
    return pl.pallas_call(
        kernel,
        grid=(m // tb,),
        in_specs=[
            pl.BlockSpec((tb, 5 * d), lambda i: (i, 0)),
            chunk_spec,
            pl.BlockSpec((tb, K_A - 1, d), lambda i: (i, 0, 0)),
            pl.BlockSpec((K_B - 1, tb, d), lambda i: (0, i, 0)),
            cache_spec, cache_spec,
            _resident((K_A, d)),
            _resident((K_B, d)),
            _resident((1, d)), _resident((1, d)), _resident((1, d)),
        ],
        out_specs=[
            row_spec, row_spec, chunk_spec,
            pl.BlockSpec((tb, K_A - 1, d), lambda i: (i, 0, 0)),
            pl.BlockSpec((K_B - 1, tb, d), lambda i: (0, i, 0)),
        ],
        out_shape=[
            jax.ShapeDtypeStruct((m, d), F32),
            jax.ShapeDtypeStruct((m, d), F32),
            jax.ShapeDtypeStruct((m, cs, lanes), F32),
            jax.ShapeDtypeStruct((m, K_A - 1, d), F32),
            jax.ShapeDtypeStruct((K_B - 1, m, d), F32),
        ],
        compiler_params=pltpu.CompilerParams(
            dimension_semantics=("arbitrary",), vmem_limit_bytes=48 * MIB),
        name="sample_token",
    )(proj, qv, sa, sb, k, v, caw, cbw, cbb, lng, lnb)


def _to_chunks(a, lanes):
    *lead, nh, dh = a.shape
    a = a.reshape(*lead, nh, dh // lanes, lanes)
    a = jnp.swapaxes(a, -3, -2)
    return a.reshape(*lead, (dh // lanes) * nh, lanes)


def _from_chunks(a, nh):
    *lead, cs, lanes = a.shape
    a = a.reshape(*lead, cs // nh, nh, lanes)
    a = jnp.swapaxes(a, -3, -2)
    return a.reshape(*lead, cs * lanes)


def _spost_kernel(x_ref, prea_ref, preb_ref, o_ref, gl_ref, bgate_ref,
                  woa_ref, wob_ref, wox_ref, wo_ref, h_ref, *, d):
    ya = _dot(prea_ref[...].astype(BF16), woa_ref[...])
    yb = _dot(preb_ref[...].astype(BF16), wob_ref[...])
    yx = _dot(o_ref[...].astype(BF16), wox_ref[...])
    g = _sigmoid(gl_ref[...] + bgate_ref[...])
    merged = g[:, :d] * ya + g[:, d:2 * d] * yb + g[:, 2 * d:] * yx
    h_ref[...] = x_ref[...] + _dot(merged.astype(BF16), wo_ref[...])


def _spost_call(x, prea, preb, o, proj, bgate, woa, wob, wox, wo):
    m, d = x.shape
    kernel = functools.partial(_spost_kernel, d=d)
    full = pl.BlockSpec((m, d), lambda i: (0, 0))
    return pl.pallas_call(
        kernel,
        grid=(1,),
        in_specs=[
            full, full, full, full,
            pl.BlockSpec((m, 3 * d), lambda i: (0, 2)),
            _resident((1, 3 * d)),
            _resident((d, d)), _resident((d, d)), _resident((d, d)), _resident((d, d)),
        ],
        out_specs=full,
        out_shape=jax.ShapeDtypeStruct((m, d), F32),
        compiler_params=pltpu.CompilerParams(
            dimension_semantics=("arbitrary",), vmem_limit_bytes=32 * MIB),
        name="sample_post",
    )(x, prea, preb, o, proj, bgate, woa, wob, wox, wo)


def kernel(x_prompt, x_sample, mem_prompt, cache_mem_k, cache_mem_v, state_conv_a, state_conv_b, norm_mix, w_in, b_gate, conv_a_w, w_out_a, conv_b_w, conv_b_bias, ln_b_g, ln_b_b, w_out_b, norm_mem, w_k, w_v, w_out_x, w_o, norm_ffn, w_ff_gate, w_ff_up, w_ff_down, norm_final):
    depth = w_in.shape[0]
    assert depth == 1, "single-layer step only"
    b, t, d = x_prompt.shape
    sb_, st_, _ = x_sample.shape
    assert st_ == 1
    n_mem = mem_prompt.shape[1]
    nh, dh = w_k.shape[2], w_k.shape[3]
    assert nh == N_HEADS and nh * dh == d

    bf = lambda w: w.astype(BF16)
    row = lambda v: v.reshape(1, -1)
    w_in_b = bf(w_in[0])
    woa, wob, wox, wo = bf(w_out_a[0]), bf(w_out_b[0]), bf(w_out_x[0]), bf(w_o[0])
    wk = bf(w_k[0].reshape(d, d))
    wv = bf(w_v[0].reshape(d, d))
    wkt = wk.T
    wg, wu, wd = bf(w_ff_gate[0]), bf(w_ff_up[0]), bf(w_ff_down[0])
    nmix, nffn, nfin, nmem = row(norm_mix[0]), row(norm_ffn[0]), row(norm_final), row(norm_mem[0])
    bgate, cbb, lng, lnb = row(b_gate[0]), row(conv_b_bias[0]), row(ln_b_g[0]), row(ln_b_b[0])
    caw, cbw = conv_a_w[0], conv_b_w[0]

    k_p, v_p, kt, vb = _kv_call(mem_prompt, nmem, wk, wv, wkt)
    h_p, ca_p, cb_p = _mixer_call(x_prompt, kt, vb, w_in_b, woa, wob, wox, wo,
                                  nmix, bgate, caw, cbw, cbb, lng, lnb)
    y_p = _ffn_call(h_p.reshape(b * t, d), nffn, wg, wu, wd, nfin, PROMPT_TILE)

    xs = x_sample.reshape(sb_, d)
    proj_s = _sproj_call(xs, nmix, w_in_b)
    assert dh == 2 * LANES and 2 * nh == SUBLANES
    q_s = proj_s[:, 5 * d:6 * d].reshape(sb_, nh, dh)
    prea, preb, o_s, ca_s, cb_s = _stoken_call(
        proj_s, _to_chunks(q_s, LANES), state_conv_a[0],
        jnp.swapaxes(state_conv_b[0], 0, 1),
        _to_chunks(cache_mem_k[0], LANES), _to_chunks(cache_mem_v[0], LANES),
        caw, cbw, cbb, lng, lnb)
    h_s = _spost_call(xs, prea, preb, _from_chunks(o_s, nh), proj_s, bgate, woa, wob, wox, wo)
    y_s = _ffn_call(h_s, nffn, wg, wu, wd, nfin, sb_)

    return (y_p.reshape(b, t, d), y_s.reshape(sb_, 1, d),
            k_p.reshape(1, b, n_mem, nh, dh), v_p.reshape(1, b, n_mem, nh, dh),
            ca_p[None], cb_p[None], ca_s[None], jnp.swapaxes(cb_s, 0, 1)[None])
```

```python
import functools

import jax
import jax.numpy as jnp
from jax import lax
from jax.experimental import pallas as pl
from jax.experimental.pallas import tpu as pltpu

F32 = jnp.float32
BF16 = jnp.bfloat16
EPS = 1e-6

K_A = 3
K_B = 31
N_HEADS = 4
SUBLANES = 8
LANES = 128
HALO_A = 8
HALO_B = 32
CONV_ROWS = 64
GLU_GROUPS = 1

PROMPT_TILE = 512
SAMPLE_TOKENS = 32
MIB = 1024 * 1024


def _dot(a, b):
    return jnp.dot(a, b, preferred_element_type=F32)


def _rms(x, g):
    ms = jnp.mean(x * x, axis=-1, keepdims=True)
    return x * lax.rsqrt(ms + EPS) * g


def _sigmoid(x):
    return 1.0 / (1.0 + jnp.exp(-x))


def _layernorm_swish(y, g, b):
    mu = jnp.mean(y, axis=-1, keepdims=True)
    yc = y - mu
    var = jnp.mean(yc * yc, axis=-1, keepdims=True)
    z = yc * lax.rsqrt(var + EPS) * g + b
    return z * _sigmoid(z)


def _resident(shape):
    nd = len(shape)
    return pl.BlockSpec(shape, lambda *_: (0,) * nd, pipeline_mode=pl.Buffered(1))


def _kv_kernel(mem_ref, g_ref, wk_ref, wv_ref, wkt_ref, k_ref, v_ref, kt_ref, vb_ref):
    mn = _rms(mem_ref[0], g_ref[...]).astype(BF16)
    k_ref[0] = _dot(mn, wk_ref[...])
    v = _dot(mn, wv_ref[...])
    v_ref[0] = v
    vb_ref[0] = v.astype(BF16)
    kt = lax.dot_general(wkt_ref[...], mn, (((1,), (1,)), ((), ())),
                         preferred_element_type=F32)
    kt_ref[0] = kt.astype(BF16)


def _kv_call(mem, g, wk, wv, wkt):
    b, n_mem, d = mem.shape
    dx = wk.shape[1]
    return pl.pallas_call(
        _kv_kernel,
        grid=(b,),
        in_specs=[
            pl.BlockSpec((1, n_mem, d), lambda i: (i, 0, 0)),
            _resident((1, d)),
            _resident((d, dx)),
            _resident((d, dx)),
            _resident((dx, d)),
        ],
        out_specs=[
            pl.BlockSpec((1, n_mem, dx), lambda i: (i, 0, 0)),
            pl.BlockSpec((1, n_mem, dx), lambda i: (i, 0, 0)),
            pl.BlockSpec((1, dx, n_mem), lambda i: (i, 0, 0)),
            pl.BlockSpec((1, n_mem, dx), lambda i: (i, 0, 0)),
        ],
        out_shape=[
            jax.ShapeDtypeStruct((b, n_mem, dx), F32),
            jax.ShapeDtypeStruct((b, n_mem, dx), F32),
            jax.ShapeDtypeStruct((b, dx, n_mem), BF16),
            jax.ShapeDtypeStruct((b, n_mem, dx), BF16),
        ],
        compiler_params=pltpu.CompilerParams(
            dimension_semantics=("arbitrary",), vmem_limit_bytes=32 * MIB),
        name="kv",
    )(mem, g, wk, wv, wkt)


def _mixer_kernel(x_ref, kt_ref, vb_ref, w_in_ref, woa_ref, wob_ref, wox_ref, wo_ref,
                  nmix_ref, bgate_ref, caw_ref, cbw_ref, cbb_ref, lng_ref, lnb_ref,
                  h_ref, na_ref, nb_ref, ua_buf, ub_buf, convb_buf, preb_buf, *, tm, d):
    t = pl.program_id(1)
    dh = d // N_HEADS

    @pl.when(t == 0)
    def _():
        ua_buf[0:HALO_A, :] = jnp.zeros((HALO_A, d), F32)
        ub_buf[0:HALO_B, :] = jnp.zeros((HALO_B, d), F32)

    x = x_ref[0]
    xn = _rms(x, nmix_ref[...]).astype(BF16)

    def proj(c0, c1):
        return _dot(xn, w_in_ref[:, c0 * d:c1 * d])

    def gate(i):
        return _sigmoid(proj(6 + i, 7 + i) + bgate_ref[:, i * d:(i + 1) * d])

    gw = d // GLU_GROUPS

    def glu_group(j):
        val = _dot(xn, w_in_ref[:, 3 * d + j * gw:3 * d + (j + 1) * gw])
        gt = _dot(xn, w_in_ref[:, 4 * d + j * gw:4 * d + (j + 1) * gw])
        ub_buf[HALO_B:HALO_B + tm, j * gw:(j + 1) * gw] = val * _sigmoid(gt)

    def conv_block(c, lb):
        r0 = c * CONV_ROWS
        win_rows = CONV_ROWS + HALO_B
        lanes = slice(lb * LANES, (lb + 1) * LANES)
        win = ub_buf[r0:r0 + win_rows, lanes]
        acc = jnp.zeros((CONV_ROWS, LANES), F32)
        for r in range(SUBLANES):
            rolled = win if r == 0 else pltpu.roll(win, win_rows - r, 0)
            for a in range(win_rows // SUBLANES):
                j = SUBLANES * a + r - (HALO_B - (K_B - 1))
                if 0 <= j < K_B:
                    rows = rolled[SUBLANES * a:SUBLANES * a + CONV_ROWS, :]
                    acc = acc + cbw_ref[j:j + 1, lanes] * rows
        convb_buf[r0:r0 + CONV_ROWS, lanes] = acc + cbb_ref[:, lanes]

    def norm_chunk(c):
        r0 = c * CONV_ROWS
        z = _layernorm_swish(convb_buf[r0:r0 + CONV_ROWS, :], lng_ref[...], lnb_ref[...])
        preb_buf[r0:r0 + CONV_ROWS, :] = z.astype(BF16)

    out = {}

    def branch_a():
        cx = proj(1, 3)
        ua_buf[HALO_A:HALO_A + tm, :] = cx[:, :d] * cx[:, d:]
        conv_a = caw_ref[0:1, :] * ua_buf[HALO_A - 2:HALO_A - 2 + tm, :]
        conv_a = conv_a + caw_ref[1:2, :] * ua_buf[HALO_A - 1:HALO_A - 1 + tm, :]
        conv_a = conv_a + caw_ref[2:3, :] * ua_buf[HALO_A:HALO_A + tm, :]
        out["pre_a"] = (proj(0, 1) * conv_a).astype(BF16)
        na_ref[0] = ua_buf[HALO_A + tm - (K_A - 1):HALO_A + tm, :]
        ua_buf[0:HALO_A, :] = ua_buf[tm:tm + HALO_A, :]

    def branch_a_out():
        out["ya"] = gate(0) * _dot(out["pre_a"], woa_ref[...])

    def query():
        out["q"] = (proj(5, 6) * (dh ** -0.5)).astype(BF16)
        out["heads"] = []

    def head(hd):
        sl = slice(hd * dh, (hd + 1) * dh)
        s = _dot(out["q"][:, sl], kt_ref[0, sl, :])
        e = jnp.exp(s - jnp.max(s, axis=-1, keepdims=True))
        p = e * (1.0 / jnp.sum(e, axis=-1, keepdims=True))
        out["heads"].append(_dot(p.astype(BF16), vb_ref[0, :, sl]))

    def branch_x_out():
        o = jnp.concatenate(out["heads"], axis=-1).astype(BF16)
        out["yx"] = gate(2) * _dot(o, wox_ref[...])

    def gate_b():
        out["gb"] = gate(1)

    mxu_work = [branch_a, branch_a_out, query,
                functools.partial(head, 0), functools.partial(head, 1),
                functools.partial(head, 2), functools.partial(head, 3),
                branch_x_out, gate_b]
    n_chunks = tm // CONV_ROWS
    vector_work = [functools.partial(glu_group, j) for j in range(GLU_GROUPS)]
    for c in range(n_chunks):
        for lb in range(d // LANES):
            vector_work.append(functools.partial(conv_block, c, lb))
        vector_work.append(functools.partial(norm_chunk, c))
    done = 0
    for i, work in enumerate(vector_work):
        work()
        while done < (i + 1) * len(mxu_work) // len(vector_work):
            mxu_work[done]()
            done += 1

    nb_ref[0] = ub_buf[HALO_B + tm - (K_B - 1):HALO_B + tm, :]
    ub_buf[0:HALO_B, :] = ub_buf[tm:tm + HALO_B, :]

    yb = _dot(preb_buf[...], wob_ref[...])
    merged = out["ya"] + out["gb"] * yb + out["yx"]
    h_ref[0] = x + _dot(merged.astype(BF16), wo_ref[...])


def _mixer_call(x, kt, vb, w_in, woa, wob, wox, wo, nmix, bgate, caw, cbw, cbb, lng, lnb):
    b, t, d = x.shape
    tm = PROMPT_TILE
    n_mem = vb.shape[1]
    d_in = w_in.shape[1]
    kernel = functools.partial(_mixer_kernel, tm=tm, d=d)
    return pl.pallas_call(
        kernel,
        grid=(b, t // tm),
        in_specs=[
            pl.BlockSpec((1, tm, d), lambda i, j: (i, j, 0)),
            pl.BlockSpec((1, d, n_mem), lambda i, j: (i, 0, 0)),
            pl.BlockSpec((1, n_mem, d), lambda i, j: (i, 0, 0)),
            _resident((d, d_in)),
            _resident((d, d)), _resident((d, d)), _resident((d, d)), _resident((d, d)),
            _resident((1, d)),
            _resident((1, 3 * d)),
            _resident((K_A, d)),
            _resident((K_B, d)),
            _resident((1, d)), _resident((1, d)), _resident((1, d)),
        ],
        out_specs=[
            pl.BlockSpec((1, tm, d), lambda i, j: (i, j, 0)),
            pl.BlockSpec((1, K_A - 1, d), lambda i, j: (i, 0, 0)),
            pl.BlockSpec((1, K_B - 1, d), lambda i, j: (i, 0, 0)),
        ],
        out_shape=[
            jax.ShapeDtypeStruct((b, t, d), F32),
            jax.ShapeDtypeStruct((b, K_A - 1, d), F32),
            jax.ShapeDtypeStruct((b, K_B - 1, d), F32),
        ],
        scratch_shapes=[
            pltpu.VMEM((HALO_A + tm, d), F32),
            pltpu.VMEM((HALO_B + tm, d), F32),
            pltpu.VMEM((tm, d), F32),
            pltpu.VMEM((tm, d), BF16),
        ],
        compiler_params=pltpu.CompilerParams(
            dimension_semantics=("arbitrary", "arbitrary"), vmem_limit_bytes=62 * MIB),
        name="mixer",
    )(x, kt, vb, w_in, woa, wob, wox, wo, nmix, bgate, caw, cbw, cbb, lng, lnb)


def _token_attention(q, k, v):
    n_mem, cs, lanes = k.shape
    scale = (cs // N_HEADS * lanes) ** -0.5
    s = jnp.sum(k * (q * scale)[None], axis=-1, keepdims=True)
    s = s + pltpu.roll(s, N_HEADS, 1)
    e = jnp.exp(s - jnp.max(s, axis=0, keepdims=True))
    den = jnp.sum(e, axis=0)
    return jnp.sum(e * v, axis=0) * (1.0 / den)


def _ffn_kernel(h_ref, nffn_ref, wg_ref, wu_ref, wd_ref, nfin_ref, *rest, attn_tokens):
    if attn_tokens:
        qv_ref, k_ref, v_ref, y_ref, o_ref = rest
        for i in range(attn_tokens):
            o_ref[i] = _token_attention(qv_ref[i], k_ref[i], v_ref[i])
    else:
        (y_ref,) = rest
    h = h_ref[...]
    hn = _rms(h, nffn_ref[...]).astype(BF16)
    gate = _dot(hn, wg_ref[...])
    up = _dot(hn, wu_ref[...])
    act = (gate * _sigmoid(gate) * up).astype(BF16)
    h2 = h + _dot(act, wd_ref[...])
    y_ref[...] = _rms(h2, nfin_ref[...])


def _ffn_call(h, nffn, wg, wu, wd, nfin, tm, attn=None):
    m, d = h.shape
    dff = wg.shape[1]
    steps = m // tm
    in_specs = [
        pl.BlockSpec((tm, d), lambda i: (i, 0)),
        _resident((1, d)),
        _resident((d, dff)), _resident((d, dff)), _resident((dff, d)),
        _resident((1, d)),
    ]
    out_specs = [pl.BlockSpec((tm, d), lambda i: (i, 0))]
    out_shape = [jax.ShapeDtypeStruct((m, d), F32)]
    args = [h, nffn, wg, wu, wd, nfin]
    attn_tokens = 0
    if attn is not None:
        qv, k, v = attn
        n_tok, n_mem, cs, lanes = k.shape
        assert n_tok % steps == 0
        attn_tokens = n_tok // steps
        chunk_spec = pl.BlockSpec((attn_tokens, cs, lanes), lambda i: (i, 0, 0))
        cache_spec = pl.BlockSpec((attn_tokens, n_mem, cs, lanes), lambda i: (i, 0, 0, 0))
        in_specs += [chunk_spec, cache_spec, cache_spec]
        out_specs.append(chunk_spec)
        out_shape.append(jax.ShapeDtypeStruct((n_tok, cs, lanes), F32))
        args += [qv, k, v]
    return pl.pallas_call(
        functools.partial(_ffn_kernel, attn_tokens=attn_tokens),
        grid=(steps,),
        in_specs=in_specs,
        out_specs=out_specs,
        out_shape=out_shape,
        compiler_params=pltpu.CompilerParams(
            dimension_semantics=("arbitrary",), vmem_limit_bytes=56 * MIB),
        name="ffn_attn" if attn_tokens else "ffn",
    )(*args)


def _sproj_kernel(x_ref, nmix_ref, w_ref, o_ref):
    xn = _rms(x_ref[...], nmix_ref[...]).astype(BF16)
    o_ref[...] = _dot(xn, w_ref[...])


def _sproj_call(x, nmix, w_in):
    m, d = x.shape
    d_in = w_in.shape[1]
    return pl.pallas_call(
        _sproj_kernel,
        grid=(d_in // d,),
        in_specs=[
            _resident((m, d)),
            _resident((1, d)),
            pl.BlockSpec((d, d), lambda j: (0, j)),
        ],
        out_specs=pl.BlockSpec((m, d), lambda j: (0, j)),
        out_shape=jax.ShapeDtypeStruct((m, d_in), F32),
        compiler_params=pltpu.CompilerParams(
            dimension_semantics=("arbitrary",), vmem_limit_bytes=32 * MIB),
        name="sample_proj",
    )(x, nmix, w_in)


def _sstate_kernel(pr_ref, sa_ref, sb_ref, caw_ref, cbw_ref, cbb_ref,
                   lng_ref, lnb_ref, prea_ref, preb_ref, na_ref, nb_ref, *, tb, d):
    a_b = pr_ref[:, 0:d]
    ua = pr_ref[:, d:2 * d] * pr_ref[:, 2 * d:3 * d]
    ub = pr_ref[:, 3 * d:4 * d] * _sigmoid(pr_ref[:, 4 * d:5 * d])

    conv_b = cbw_ref[K_B - 1:K_B, :] * ub + cbb_ref[...]
    for j in range(K_B - 1):
        conv_b = conv_b + cbw_ref[j:j + 1, :] * sb_ref[j]
    preb_ref[...] = _layernorm_swish(conv_b, lng_ref[...], lnb_ref[...])
    for j in range(K_B - 2):
        nb_ref[j] = sb_ref[j + 1]
    nb_ref[K_B - 2] = ub

    for i in range(tb):
        row = slice(i, i + 1)
        conv_a = (caw_ref[0:1, :] * sa_ref[i, 0:1, :] + caw_ref[1:2, :] * sa_ref[i, 1:2, :]
                  + caw_ref[2:3, :] * ua[row])
        prea_ref[row, :] = a_b[row] * conv_a
        na_ref[i, 0:1, :] = sa_ref[i, 1:2, :]
        na_ref[i, 1:2, :] = ua[row]


def _sstate_call(proj, sa, sb, caw, cbw, cbb, lng, lnb):
    m = proj.shape[0]
    d = sb.shape[2]
    tb = SAMPLE_TOKENS
    kernel = functools.partial(_sstate_kernel, tb=tb, d=d)
    row_spec = pl.BlockSpec((tb, d), lambda i: (i, 0))
    return pl.pallas_call(
        kernel,
        grid=(m // tb,),
        in_specs=[
            pl.BlockSpec((tb, 5 * d), lambda i: (i, 0)),
            pl.BlockSpec((tb, K_A - 1, d), lambda i: (i, 0, 0)),
            pl.BlockSpec((K_B - 1, tb, d), lambda i: (0, i, 0)),
            _resident((K_A, d)),
            _resident((K_B, d)),
            _resident((1, d)), _resident((1, d)), _resident((1, d)),
        ],
        out_specs=[
            row_spec, row_spec,
            pl.BlockSpec((tb, K_A - 1, d), lambda i: (i, 0, 0)),
            pl.BlockSpec((K_B - 1, tb, d), lambda i: (0, i, 0)),
        ],
        out_shape=[
            jax.ShapeDtypeStruct((m, d), F32),
            jax.ShapeDtypeStruct((m, d), F32),
            jax.ShapeDtypeStruct((m, K_A - 1, d), F32),
            jax.ShapeDtypeStruct((K_B - 1, m, d), F32),
        ],
        compiler_params=pltpu.CompilerParams(
            dimension_semantics=("arbitrary",), vmem_limit_bytes=32 * MIB),
        name="sample_state",
    )(proj, sa, sb, caw, cbw, cbb, lng, lnb)


def _to_chunks(a, lanes):
    *lead, nh, dh = a.shape
    a = a.reshape(*lead, nh, dh // lanes, lanes)
    a = jnp.swapaxes(a, -3, -2)
    return a.reshape(*lead, (dh // lanes) * nh, lanes)


def _from_chunks(a, nh):
    *lead, cs, lanes = a.shape
    a = a.reshape(*lead, cs // nh, nh, lanes)
    a = jnp.swapaxes(a, -3, -2)
    return a.reshape(*lead, cs * lanes)


def _spost_kernel(x_ref, prea_ref, preb_ref, o_ref, gl_ref, bgate_ref,
                  woa_ref, wob_ref, wox_ref, wo_ref, h_ref, *, d):
    ya = _dot(prea_ref[...].astype(BF16), woa_ref[...])
    yb = _dot(preb_ref[...].astype(BF16), wob_ref[...])
    yx = _dot(o_ref[...].astype(BF16), wox_ref[...])
    g = _sigmoid(gl_ref[...] + bgate_ref[...])
    merged = g[:, :d] * ya + g[:, d:2 * d] * yb + g[:, 2 * d:] * yx
    h_ref[...] = x_ref[...] + _dot(merged.astype(BF16), wo_ref[...])


def _spost_call(x, prea, preb, o, proj, bgate, woa, wob, wox, wo):
    m, d = x.shape
    kernel = functools.partial(_spost_kernel, d=d)
    full = pl.BlockSpec((m, d), lambda i: (0, 0))
    return pl.pallas_call(
        kernel,
        grid=(1,),
        in_specs=[
            full, full, full, full,
            pl.BlockSpec((m, 3 * d), lambda i: (0, 2)),
            _resident((1, 3 * d)),
            _resident((d, d)), _resident((d, d)), _resident((d, d)), _resident((d, d)),
        ],
        out_specs=full,
        out_shape=jax.ShapeDtypeStruct((m, d), F32),
        compiler_params=pltpu.CompilerParams(
            dimension_semantics=("arbitrary",), vmem_limit_bytes=32 * MIB),
        name="sample_post",
    )(x, prea, preb, o, proj, bgate, woa, wob, wox, wo)


def kernel(x_prompt, x_sample, mem_prompt, cache_mem_k, cache_mem_v, state_conv_a, state_conv_b, norm_mix, w_in, b_gate, conv_a_w, w_out_a, conv_b_w, conv_b_bias, ln_b_g, ln_b_b, w_out_b, norm_mem, w_k, w_v, w_out_x, w_o, norm_ffn, w_ff_gate, w_ff_up, w_ff_down, norm_final):
    depth = w_in.shape[0]
    assert depth == 1, "single-layer step only"
    b, t, d = x_prompt.shape
    sb_, st_, _ = x_sample.shape
    assert st_ == 1
    n_mem = mem_prompt.shape[1]
    nh, dh = w_k.shape[2], w_k.shape[3]
    assert nh == N_HEADS and nh * dh == d

    bf = lambda w: w.astype(BF16)
    row = lambda v: v.reshape(1, -1)
    w_in_b = bf(w_in[0])
    woa, wob, wox, wo = bf(w_out_a[0]), bf(w_out_b[0]), bf(w_out_x[0]), bf(w_o[0])
    wk = bf(w_k[0].reshape(d, d))
    wv = bf(w_v[0].reshape(d, d))
    wkt = wk.T
    wg, wu, wd = bf(w_ff_gate[0]), bf(w_ff_up[0]), bf(w_ff_down[0])
    nmix, nffn, nfin, nmem = row(norm_mix[0]), row(norm_ffn[0]), row(norm_final), row(norm_mem[0])
    bgate, cbb, lng, lnb = row(b_gate[0]), row(conv_b_bias[0]), row(ln_b_g[0]), row(ln_b_b[0])
    caw, cbw = conv_a_w[0], conv_b_w[0]

    k_p, v_p, kt, vb = _kv_call(mem_prompt, nmem, wk, wv, wkt)
    h_p, ca_p, cb_p = _mixer_call(x_prompt, kt, vb, w_in_b, woa, wob, wox, wo,
                                  nmix, bgate, caw, cbw, cbb, lng, lnb)

    assert dh == 2 * LANES and 2 * nh == SUBLANES
    xs = x_sample.reshape(sb_, d)
    proj_s = _sproj_call(xs, nmix, w_in_b)
    q_s = _to_chunks(proj_s[:, 5 * d:6 * d].reshape(sb_, nh, dh), LANES)
    prea, preb, ca_s, cb_s = _sstate_call(
        proj_s, state_conv_a[0], jnp.swapaxes(state_conv_b[0], 0, 1), caw, cbw, cbb, lng, lnb)

    y_p, o_s = _ffn_call(
        h_p.reshape(b * t, d), nffn, wg, wu, wd, nfin, PROMPT_TILE,
        attn=(q_s, _to_chunks(cache_mem_k[0], LANES), _to_chunks(cache_mem_v[0], LANES)))

    h_s = _spost_call(xs, prea, preb, _from_chunks(o_s, nh), proj_s, bgate, woa, wob, wox, wo)
    (y_s,) = _ffn_call(h_s, nffn, wg, wu, wd, nfin, sb_)

    return (y_p.reshape(b, t, d), y_s.reshape(sb_, 1, d),
            k_p.reshape(1, b, n_mem, nh, dh), v_p.reshape(1, b, n_mem, nh, dh),
            ca_p[None], cb_p[None], ca_s[None], jnp.swapaxes(cb_s, 0, 1)[None])
```

```python
import functools

import jax
import jax.numpy as jnp
from jax import lax
from jax.experimental import pallas as pl
from jax.experimental.pallas import tpu as pltpu

F32 = jnp.float32
BF16 = jnp.bfloat16
EPS = 1e-6
LOG2_E = 1.4426950408889634

K_A = 3
K_B = 31
N_HEADS = 4
SUBLANES = 8
LANES = 128
HALO_A = 8
HALO_B = 32
CONV_ROWS = 64

PROMPT_TILE = 512
SAMPLE_TOKENS = 32
MIB = 1024 * 1024


def _dot(a, b):
    return jnp.dot(a, b, preferred_element_type=F32)


def _rms(x, g):
    ms = jnp.mean(x * x, axis=-1, keepdims=True)
    return x * lax.rsqrt(ms + EPS) * g


def _sigmoid(x):
    return 1.0 / (1.0 + jnp.exp2(x * (-LOG2_E)))


def _layernorm_swish(y, g, b):
    mu = jnp.mean(y, axis=-1, keepdims=True)
    yc = y - mu
    var = jnp.mean(yc * yc, axis=-1, keepdims=True)
    z = yc * lax.rsqrt(var + EPS) * g + b
    return z * _sigmoid(z)


def _resident(shape):
    nd = len(shape)
    return pl.BlockSpec(shape, lambda *_: (0,) * nd, pipeline_mode=pl.Buffered(1))


def _kv_kernel(mem_ref, g_ref, wk_ref, wv_ref, k_ref, v_ref, kt_ref, vb_ref):
    mn = _rms(mem_ref[0], g_ref[...]).astype(BF16)
    k = _dot(mn, wk_ref[...])
    v = _dot(mn, wv_ref[...])
    vb_ref[0] = v.astype(BF16)
    kt_ref[0] = k.T.astype(BF16)
    cs, lanes = k_ref.shape[2:]
    halves = cs // N_HEADS
    dh = halves * lanes
    for half in range(halves):
        for hd in range(N_HEADS):
            cols = slice(hd * dh + half * lanes, hd * dh + (half + 1) * lanes)
            k_ref[0, :, half * N_HEADS + hd, :] = k[:, cols]
            v_ref[0, :, half * N_HEADS + hd, :] = v[:, cols]


def _kv_call(mem, g, wk, wv):
    b, n_mem, d = mem.shape
    dx = wk.shape[1]
    cs = dx // LANES
    chunk_spec = pl.BlockSpec((1, n_mem, cs, LANES), lambda i: (i, 0, 0, 0))
    return pl.pallas_call(
        _kv_kernel,
        grid=(b,),
        in_specs=[
            pl.BlockSpec((1, n_mem, d), lambda i: (i, 0, 0)),
            _resident((1, d)),
            _resident((d, dx)),
            _resident((d, dx)),
        ],
        out_specs=[
            chunk_spec, chunk_spec,
            pl.BlockSpec((1, dx, n_mem), lambda i: (i, 0, 0)),
            pl.BlockSpec((1, n_mem, dx), lambda i: (i, 0, 0)),
        ],
        out_shape=[
            jax.ShapeDtypeStruct((b, n_mem, cs, LANES), F32),
            jax.ShapeDtypeStruct((b, n_mem, cs, LANES), F32),
            jax.ShapeDtypeStruct((b, dx, n_mem), BF16),
            jax.ShapeDtypeStruct((b, n_mem, dx), BF16),
        ],
        compiler_params=pltpu.CompilerParams(
            dimension_semantics=("arbitrary",), vmem_limit_bytes=32 * MIB),
        name="kv",
    )(mem, g, wk, wv)


def _mixer_kernel(x_ref, kt_ref, vb_ref, w_in_ref, woa_ref, wob_ref, wox_ref, wo_ref,
                  nmix_ref, bgate_ref, caw_ref, cbw_ref, cbb_ref, lng_ref, lnb_ref,
                  h_ref, na_ref, nb_ref, ua_buf, ub_buf, convb_buf, preb_buf, *, tm, d):
    t = pl.program_id(1)
    dh = d // N_HEADS

    @pl.when(t == 0)
    def _():
        ua_buf[0:HALO_A, :] = jnp.zeros((HALO_A, d), F32)
        ub_buf[0:HALO_B, :] = jnp.zeros((HALO_B, d), F32)

    x = x_ref[0]
    xn = _rms(x, nmix_ref[...]).astype(BF16)

    def proj(c0, c1):
        return _dot(xn, w_in_ref[:, c0 * d:c1 * d])

    def gate(i):
        return _sigmoid(proj(6 + i, 7 + i) + bgate_ref[:, i * d:(i + 1) * d])

    def glu():
        ub_buf[HALO_B:HALO_B + tm, :] = proj(3, 4) * _sigmoid(proj(4, 5))

    def conv_block(c, lb):
        r0 = c * CONV_ROWS
        win_rows = CONV_ROWS + HALO_B
        lanes = slice(lb * LANES, (lb + 1) * LANES)
        win = ub_buf[r0:r0 + win_rows, lanes]
        acc = jnp.zeros((CONV_ROWS, LANES), F32)
        for r in range(SUBLANES):
            rolled = win if r == 0 else pltpu.roll(win, win_rows - r, 0)
            for a in range(win_rows // SUBLANES):
                j = SUBLANES * a + r - (HALO_B - (K_B - 1))
                if 0 <= j < K_B:
                    rows = rolled[SUBLANES * a:SUBLANES * a + CONV_ROWS, :]
                    acc = acc + cbw_ref[j:j + 1, lanes] * rows
        convb_buf[r0:r0 + CONV_ROWS, lanes] = acc + cbb_ref[:, lanes]

    def norm_chunk(c):
        r0 = c * CONV_ROWS
        z = _layernorm_swish(convb_buf[r0:r0 + CONV_ROWS, :], lng_ref[...], lnb_ref[...])
        preb_buf[r0:r0 + CONV_ROWS, :] = z.astype(BF16)

    out = {}

    def branch_a():
        cx = proj(1, 3)
        ua_buf[HALO_A:HALO_A + tm, :] = cx[:, :d] * cx[:, d:]
        conv_a = caw_ref[0:1, :] * ua_buf[HALO_A - 2:HALO_A - 2 + tm, :]
        conv_a = conv_a + caw_ref[1:2, :] * ua_buf[HALO_A - 1:HALO_A - 1 + tm, :]
        conv_a = conv_a + caw_ref[2:3, :] * ua_buf[HALO_A:HALO_A + tm, :]
        out["pre_a"] = (proj(0, 1) * conv_a).astype(BF16)
        na_ref[0] = ua_buf[HALO_A + tm - (K_A - 1):HALO_A + tm, :]
        ua_buf[0:HALO_A, :] = ua_buf[tm:tm + HALO_A, :]

    def branch_a_out():
        out["ya"] = gate(0) * _dot(out["pre_a"], woa_ref[...])

    def query():
        out["q"] = (proj(5, 6) * (dh ** -0.5)).astype(BF16)
        out["heads"] = []

    def head(hd):
        sl = slice(hd * dh, (hd + 1) * dh)
        s = _dot(out["q"][:, sl], kt_ref[0, sl, :])
        e = jnp.exp(s - jnp.max(s, axis=-1, keepdims=True))
        p = e * (1.0 / jnp.sum(e, axis=-1, keepdims=True))
        out["heads"].append(_dot(p.astype(BF16), vb_ref[0, :, sl]))

    def branch_x_out():
        o = jnp.concatenate(out["heads"], axis=-1).astype(BF16)
        out["yx"] = gate(2) * _dot(o, wox_ref[...])

    def gate_b():
        out["gb"] = gate(1)

    mxu_work = [branch_a, branch_a_out, query,
                functools.partial(head, 0), functools.partial(head, 1),
                functools.partial(head, 2), functools.partial(head, 3),
                branch_x_out, gate_b]
    n_chunks = tm // CONV_ROWS
    vector_work = [glu]
    for c in range(n_chunks):
        for lb in range(d // LANES):
            vector_work.append(functools.partial(conv_block, c, lb))
        vector_work.append(functools.partial(norm_chunk, c))
    done = 0
    for i, work in enumerate(vector_work):
        work()
        while done < (i + 1) * len(mxu_work) // len(vector_work):
            mxu_work[done]()
            done += 1

    nb_ref[0] = ub_buf[HALO_B + tm - (K_B - 1):HALO_B + tm, :]
    ub_buf[0:HALO_B, :] = ub_buf[tm:tm + HALO_B, :]

    yb = _dot(preb_buf[...], wob_ref[...])
    merged = out["ya"] + out["gb"] * yb + out["yx"]
    h_ref[0] = x + _dot(merged.astype(BF16), wo_ref[...])


def _mixer_call(x, kt, vb, w_in, woa, wob, wox, wo, nmix, bgate, caw, cbw, cbb, lng, lnb):
    b, t, d = x.shape
    tm = PROMPT_TILE
    n_mem = vb.shape[1]
    d_in = w_in.shape[1]
    kernel = functools.partial(_mixer_kernel, tm=tm, d=d)
    return pl.pallas_call(
        kernel,
        grid=(b, t // tm),
        in_specs=[
            pl.BlockSpec((1, tm, d), lambda i, j: (i, j, 0)),
            pl.BlockSpec((1, d, n_mem), lambda i, j: (i, 0, 0)),
            pl.BlockSpec((1, n_mem, d), lambda i, j: (i, 0, 0)),
            _resident((d, d_in)),
            _resident((d, d)), _resident((d, d)), _resident((d, d)), _resident((d, d)),
            _resident((1, d)),
            _resident((1, 3 * d)),
            _resident((K_A, d)),
            _resident((K_B, d)),
            _resident((1, d)), _resident((1, d)), _resident((1, d)),
        ],
        out_specs=[
            pl.BlockSpec((1, tm, d), lambda i, j: (i, j, 0)),
            pl.BlockSpec((1, K_A - 1, d), lambda i, j: (i, 0, 0)),
            pl.BlockSpec((1, K_B - 1, d), lambda i, j: (i, 0, 0)),
        ],
        out_shape=[
            jax.ShapeDtypeStruct((b, t, d), F32),
            jax.ShapeDtypeStruct((b, K_A - 1, d), F32),
            jax.ShapeDtypeStruct((b, K_B - 1, d), F32),
        ],
        scratch_shapes=[
            pltpu.VMEM((HALO_A + tm, d), F32),
            pltpu.VMEM((HALO_B + tm, d), F32),
            pltpu.VMEM((tm, d), F32),
            pltpu.VMEM((tm, d), BF16),
        ],
        compiler_params=pltpu.CompilerParams(
            dimension_semantics=("arbitrary", "arbitrary"), vmem_limit_bytes=62 * MIB),
        name="mixer",
    )(x, kt, vb, w_in, woa, wob, wox, wo, nmix, bgate, caw, cbw, cbb, lng, lnb)


def _token_attention(q, k, v):
    n_mem, cs, lanes = k.shape
    scale = (cs // N_HEADS * lanes) ** -0.5
    s = jnp.sum(k * (q * scale)[None], axis=-1, keepdims=True)
    s = s + pltpu.roll(s, N_HEADS, 1)
    e = jnp.exp(s - jnp.max(s, axis=0, keepdims=True))
    den = jnp.sum(e, axis=0)
    return jnp.sum(e * v, axis=0) * (1.0 / den)


def _ffn_kernel(h_ref, nffn_ref, wg_ref, wu_ref, wd_ref, nfin_ref, *rest, attn_tokens):
    if attn_tokens:
        qv_ref, k_ref, v_ref, y_ref, o_ref = rest
        for i in range(attn_tokens):
            o_ref[i] = _token_attention(qv_ref[i], k_ref[i], v_ref[i])
    else:
        (y_ref,) = rest
    h = h_ref[...]
    hn = _rms(h, nffn_ref[...]).astype(BF16)
    gate = _dot(hn, wg_ref[...])
    up = _dot(hn, wu_ref[...])
    act = (gate * _sigmoid(gate) * up).astype(BF16)
    h2 = h + _dot(act, wd_ref[...])
    y_ref[...] = _rms(h2, nfin_ref[...])


def _ffn_call(h, nffn, wg, wu, wd, nfin, tm, attn=None):
    m, d = h.shape
    dff = wg.shape[1]
    steps = m // tm
    in_specs = [
        pl.BlockSpec((tm, d), lambda i: (i, 0)),
        _resident((1, d)),
        _resident((d, dff)), _resident((d, dff)), _resident((dff, d)),
        _resident((1, d)),
    ]
    out_specs = [pl.BlockSpec((tm, d), lambda i: (i, 0))]
    out_shape = [jax.ShapeDtypeStruct((m, d), F32)]
    args = [h, nffn, wg, wu, wd, nfin]
    attn_tokens = 0
    if attn is not None:
        qv, k, v = attn
        n_tok, n_mem, cs, lanes = k.shape
        assert n_tok % steps == 0
        attn_tokens = n_tok // steps
        chunk_spec = pl.BlockSpec((attn_tokens, cs, lanes), lambda i: (i, 0, 0))
        cache_spec = pl.BlockSpec((attn_tokens, n_mem, cs, lanes), lambda i: (i, 0, 0, 0))
        in_specs += [chunk_spec, cache_spec, cache_spec]
        out_specs.append(chunk_spec)
        out_shape.append(jax.ShapeDtypeStruct((n_tok, cs, lanes), F32))
        args += [qv, k, v]
    return pl.pallas_call(
        functools.partial(_ffn_kernel, attn_tokens=attn_tokens),
        grid=(steps,),
        in_specs=in_specs,
        out_specs=out_specs,
        out_shape=out_shape,
        compiler_params=pltpu.CompilerParams(
            dimension_semantics=("arbitrary",), vmem_limit_bytes=56 * MIB),
        name="ffn_attn" if attn_tokens else "ffn",
    )(*args)


def _sproj_kernel(x_ref, nmix_ref, w_ref, o_ref):
    xn = _rms(x_ref[...], nmix_ref[...]).astype(BF16)
    o_ref[...] = _dot(xn, w_ref[...])


def _sproj_call(x, nmix, w_in):
    m, d = x.shape
    d_in = w_in.shape[1]
    return pl.pallas_call(
        _sproj_kernel,
        grid=(d_in // d,),
        in_specs=[
            _resident((m, d)),
            _resident((1, d)),
            pl.BlockSpec((d, d), lambda j: (0, j)),
        ],
        out_specs=pl.BlockSpec((m, d), lambda j: (0, j)),
        out_shape=jax.ShapeDtypeStruct((m, d_in), F32),
        compiler_params=pltpu.CompilerParams(
            dimension_semantics=("arbitrary",), vmem_limit_bytes=32 * MIB),
        name="sample_proj",
    )(x, nmix, w_in)


def _sstate_kernel(pr_ref, sa_ref, sb_ref, caw_ref, cbw_ref, cbb_ref,
                   lng_ref, lnb_ref, prea_ref, preb_ref, na_ref, nb_ref, *, tb, d):
    a_b = pr_ref[:, 0:d]
    ua = pr_ref[:, d:2 * d] * pr_ref[:, 2 * d:3 * d]
    ub = pr_ref[:, 3 * d:4 * d] * _sigmoid(pr_ref[:, 4 * d:5 * d])

    conv_b = cbw_ref[K_B - 1:K_B, :] * ub + cbb_ref[...]
    for j in range(K_B - 1):
        conv_b = conv_b + cbw_ref[j:j + 1, :] * sb_ref[j]
    preb_ref[...] = _layernorm_swish(conv_b, lng_ref[...], lnb_ref[...])
    for j in range(K_B - 2):
        nb_ref[j] = sb_ref[j + 1]
    nb_ref[K_B - 2] = ub

    for i in range(tb):
        row = slice(i, i + 1)
        conv_a = (caw_ref[0:1, :] * sa_ref[i, 0:1, :] + caw_ref[1:2, :] * sa_ref[i, 1:2, :]
                  + caw_ref[2:3, :] * ua[row])
        prea_ref[row, :] = a_b[row] * conv_a
        na_ref[i, 0:1, :] = sa_ref[i, 1:2, :]
        na_ref[i, 1:2, :] = ua[row]


def _sstate_call(proj, sa, sb, caw, cbw, cbb, lng, lnb):
    m = proj.shape[0]
    d = sb.shape[2]
    tb = SAMPLE_TOKENS
    kernel = functools.partial(_sstate_kernel, tb=tb, d=d)
    row_spec = pl.BlockSpec((tb, d), lambda i: (i, 0))
    return pl.pallas_call(
        kernel,
        grid=(m // tb,),
        in_specs=[
            pl.BlockSpec((tb, 5 * d), lambda i: (i, 0)),
            pl.BlockSpec((tb, K_A - 1, d), lambda i: (i, 0, 0)),
            pl.BlockSpec((K_B - 1, tb, d), lambda i: (0, i, 0)),
            _resident((K_A, d)),
            _resident((K_B, d)),
            _resident((1, d)), _resident((1, d)), _resident((1, d)),
        ],
        out_specs=[
            row_spec, row_spec,
            pl.BlockSpec((tb, K_A - 1, d), lambda i: (i, 0, 0)),
            pl.BlockSpec((K_B - 1, tb, d), lambda i: (0, i, 0)),
        ],
        out_shape=[
            jax.ShapeDtypeStruct((m, d), F32),
            jax.ShapeDtypeStruct((m, d), F32),
            jax.ShapeDtypeStruct((m, K_A - 1, d), F32),
            jax.ShapeDtypeStruct((K_B - 1, m, d), F32),
        ],
        compiler_params=pltpu.CompilerParams(
            dimension_semantics=("arbitrary",), vmem_limit_bytes=32 * MIB),
        name="sample_state",
    )(proj, sa, sb, caw, cbw, cbb, lng, lnb)


def _to_chunks(a, lanes):
    *lead, nh, dh = a.shape
    a = a.reshape(*lead, nh, dh // lanes, lanes)
    a = jnp.swapaxes(a, -3, -2)
    return a.reshape(*lead, (dh // lanes) * nh, lanes)


def _from_chunks(a, nh):
    *lead, cs, lanes = a.shape
    a = a.reshape(*lead, cs // nh, nh, lanes)
    a = jnp.swapaxes(a, -3, -2)
    return a.reshape(*lead, nh, (cs // nh) * lanes)


def _spost_kernel(x_ref, prea_ref, preb_ref, o_ref, gl_ref, bgate_ref,
                  woa_ref, wob_ref, wox_ref, wo_ref, h_ref, *, d):
    ya = _dot(prea_ref[...].astype(BF16), woa_ref[...])
    yb = _dot(preb_ref[...].astype(BF16), wob_ref[...])
    yx = _dot(o_ref[...].astype(BF16), wox_ref[...])
    g = _sigmoid(gl_ref[...] + bgate_ref[...])
    merged = g[:, :d] * ya + g[:, d:2 * d] * yb + g[:, 2 * d:] * yx
    h_ref[...] = x_ref[...] + _dot(merged.astype(BF16), wo_ref[...])


def _spost_call(x, prea, preb, o, proj, bgate, woa, wob, wox, wo):
    m, d = x.shape
    kernel = functools.partial(_spost_kernel, d=d)
    full = pl.BlockSpec((m, d), lambda i: (0, 0))
    return pl.pallas_call(
        kernel,
        grid=(1,),
        in_specs=[
            full, full, full, full,
            pl.BlockSpec((m, 3 * d), lambda i: (0, 2)),
            _resident((1, 3 * d)),
            _resident((d, d)), _resident((d, d)), _resident((d, d)), _resident((d, d)),
        ],
        out_specs=full,
        out_shape=jax.ShapeDtypeStruct((m, d), F32),
        compiler_params=pltpu.CompilerParams(
            dimension_semantics=("arbitrary",), vmem_limit_bytes=32 * MIB),
        name="sample_post",
    )(x, prea, preb, o, proj, bgate, woa, wob, wox, wo)


def kernel(x_prompt, x_sample, mem_prompt, cache_mem_k, cache_mem_v, state_conv_a, state_conv_b, norm_mix, w_in, b_gate, conv_a_w, w_out_a, conv_b_w, conv_b_bias, ln_b_g, ln_b_b, w_out_b, norm_mem, w_k, w_v, w_out_x, w_o, norm_ffn, w_ff_gate, w_ff_up, w_ff_down, norm_final):
    depth = w_in.shape[0]
    assert depth == 1, "single-layer step only"
    b, t, d = x_prompt.shape
    sb_, st_, _ = x_sample.shape
    assert st_ == 1
    n_mem = mem_prompt.shape[1]
    nh, dh = w_k.shape[2], w_k.shape[3]
    assert nh == N_HEADS and nh * dh == d
    assert dh == 2 * LANES and 2 * nh == SUBLANES

    bf = lambda w: w.astype(BF16)
    row = lambda v: v.reshape(1, -1)
    w_in_b = bf(w_in[0])
    woa, wob, wox, wo = bf(w_out_a[0]), bf(w_out_b[0]), bf(w_out_x[0]), bf(w_o[0])
    wk = bf(w_k[0].reshape(d, d))
    wv = bf(w_v[0].reshape(d, d))
    wg, wu, wd = bf(w_ff_gate[0]), bf(w_ff_up[0]), bf(w_ff_down[0])
    nmix, nffn, nfin, nmem = row(norm_mix[0]), row(norm_ffn[0]), row(norm_final), row(norm_mem[0])
    bgate, cbb, lng, lnb = row(b_gate[0]), row(conv_b_bias[0]), row(ln_b_g[0]), row(ln_b_b[0])
    caw, cbw = conv_a_w[0], conv_b_w[0]

    k_p, v_p, kt, vb = _kv_call(mem_prompt, nmem, wk, wv)
    h_p, ca_p, cb_p = _mixer_call(x_prompt, kt, vb, w_in_b, woa, wob, wox, wo,
                                  nmix, bgate, caw, cbw, cbb, lng, lnb)

    xs = x_sample.reshape(sb_, d)
    proj_s = _sproj_call(xs, nmix, w_in_b)
    q_s = _to_chunks(proj_s[:, 5 * d:6 * d].reshape(sb_, nh, dh), LANES)
    prea, preb, ca_s, cb_s = _sstate_call(
        proj_s, state_conv_a[0], jnp.swapaxes(state_conv_b[0], 0, 1), caw, cbw, cbb, lng, lnb)

    y_p, o_s = _ffn_call(
        h_p.reshape(b * t, d), nffn, wg, wu, wd, nfin, PROMPT_TILE,
        attn=(q_s, _to_chunks(cache_mem_k[0], LANES), _to_chunks(cache_mem_v[0], LANES)))

    h_s = _spost_call(xs, prea, preb, _from_chunks(o_s, nh).reshape(sb_, d), proj_s,
                      bgate, woa, wob, wox, wo)
    (y_s,) = _ffn_call(h_s, nffn, wg, wu, wd, nfin, sb_)

    return (y_p.reshape(b, t, d), y_s.reshape(sb_, 1, d),
            _from_chunks(k_p, nh)[None], _from_chunks(v_p, nh)[None],
            ca_p[None], cb_p[None], ca_s[None], jnp.swapaxes(cb_s, 0, 1)[None])
```

```python
import functools

import jax
import jax.numpy as jnp
from jax import lax
from jax.experimental import pallas as pl
from jax.experimental.pallas import tpu as pltpu

F32 = jnp.float32
BF16 = jnp.bfloat16
EPS = 1e-6
LOG2_E = 1.4426950408889634

K_A = 3
K_B = 31
N_HEADS = 4
SUBLANES = 8
LANES = 128
HALO_A = 8
HALO_B = 32
CONV_ROWS = 64

PROMPT_TILE = 512
SAMPLE_TOKENS = 32
MIB = 1024 * 1024


def _dot(a, b):
    return jnp.dot(a, b, preferred_element_type=F32)


def _rms(x, g):
    ms = jnp.mean(x * x, axis=-1, keepdims=True)
    return x * lax.rsqrt(ms + EPS) * g


def _sigmoid(x):
    return 1.0 / (1.0 + jnp.exp2(x * (-LOG2_E)))


def _layernorm_swish(y, g, b):
    mu = jnp.mean(y, axis=-1, keepdims=True)
    yc = y - mu
    var = jnp.mean(yc * yc, axis=-1, keepdims=True)
    z = yc * lax.rsqrt(var + EPS) * g + b
    return z * _sigmoid(z)


def _resident(shape):
    nd = len(shape)
    return pl.BlockSpec(shape, lambda *_: (0,) * nd, pipeline_mode=pl.Buffered(1))


def _kv_kernel(mem_ref, g_ref, wk_ref, wv_ref, *rest, n_cast):
    cast_in, (k_ref, v_ref, kt_ref, vb_ref), cast_out = (
        rest[:n_cast], rest[n_cast:n_cast + 4], rest[n_cast + 4:])
    for src, dst in zip(cast_in, cast_out):
        dst[...] = src[...].astype(BF16)
    mn = _rms(mem_ref[0], g_ref[...]).astype(BF16)
    k = _dot(mn, wk_ref[...])
    v = _dot(mn, wv_ref[...])
    vb_ref[0] = v.astype(BF16)
    kt_ref[0] = k.T.astype(BF16)
    cs, lanes = k_ref.shape[2:]
    halves = cs // N_HEADS
    dh = halves * lanes
    for half in range(halves):
        for hd in range(N_HEADS):
            cols = slice(hd * dh + half * lanes, hd * dh + (half + 1) * lanes)
            k_ref[0, :, half * N_HEADS + hd, :] = k[:, cols]
            v_ref[0, :, half * N_HEADS + hd, :] = v[:, cols]


def _kv_call(mem, g, wk, wv, cast_weights):
    b, n_mem, d = mem.shape
    dx = wk.shape[1]
    cs = dx // LANES
    chunk_spec = pl.BlockSpec((1, n_mem, cs, LANES), lambda i: (i, 0, 0, 0))
    bf16_rows = 2 * SUBLANES
    cast_specs = []
    for w in cast_weights:
        rows, cols = w.shape
        assert rows % (b * bf16_rows) == 0
        cast_specs.append(pl.BlockSpec((rows // b, cols), lambda i: (i, 0)))
    outs = pl.pallas_call(
        functools.partial(_kv_kernel, n_cast=len(cast_weights)),
        grid=(b,),
        in_specs=[
            pl.BlockSpec((1, n_mem, d), lambda i: (i, 0, 0)),
            _resident((1, d)),
            _resident((d, dx)),
            _resident((d, dx)),
        ] + cast_specs,
        out_specs=[
            chunk_spec, chunk_spec,
            pl.BlockSpec((1, dx, n_mem), lambda i: (i, 0, 0)),
            pl.BlockSpec((1, n_mem, dx), lambda i: (i, 0, 0)),
        ] + cast_specs,
        out_shape=[
            jax.ShapeDtypeStruct((b, n_mem, cs, LANES), F32),
            jax.ShapeDtypeStruct((b, n_mem, cs, LANES), F32),
            jax.ShapeDtypeStruct((b, dx, n_mem), BF16),
            jax.ShapeDtypeStruct((b, n_mem, dx), BF16),
        ] + [jax.ShapeDtypeStruct(w.shape, BF16) for w in cast_weights],
        compiler_params=pltpu.CompilerParams(
            dimension_semantics=("arbitrary",), vmem_limit_bytes=48 * MIB),
        name="kv",
    )(mem, g, wk, wv, *cast_weights)
    return outs[:4], outs[4:]


def _mixer_kernel(x_ref, kt_ref, vb_ref, w_in_ref, woa_ref, wob_ref, wox_ref, wo_ref,
                  nmix_ref, bgate_ref, caw_ref, cbw_ref, cbb_ref, lng_ref, lnb_ref,
                  h_ref, na_ref, nb_ref, ua_buf, ub_buf, convb_buf, preb_buf, *, tm, d):
    t = pl.program_id(1)
    dh = d // N_HEADS

    @pl.when(t == 0)
    def _():
        ua_buf[0:HALO_A, :] = jnp.zeros((HALO_A, d), F32)
        ub_buf[0:HALO_B, :] = jnp.zeros((HALO_B, d), F32)

    x = x_ref[0]
    xn = _rms(x, nmix_ref[...]).astype(BF16)

    def proj(c0, c1):
        return _dot(xn, w_in_ref[:, c0 * d:c1 * d])

    def gate(i):
        return _sigmoid(proj(6 + i, 7 + i) + bgate_ref[:, i * d:(i + 1) * d])

    def glu():
        ub_buf[HALO_B:HALO_B + tm, :] = proj(3, 4) * _sigmoid(proj(4, 5))

    def conv_block(c, lb):
        r0 = c * CONV_ROWS
        win_rows = CONV_ROWS + HALO_B
        lanes = slice(lb * LANES, (lb + 1) * LANES)
        win = ub_buf[r0:r0 + win_rows, lanes]
        acc = jnp.zeros((CONV_ROWS, LANES), F32)
        for r in range(SUBLANES):
            rolled = win if r == 0 else pltpu.roll(win, win_rows - r, 0)
            for a in range(win_rows // SUBLANES):
                j = SUBLANES * a + r - (HALO_B - (K_B - 1))
                if 0 <= j < K_B:
                    rows = rolled[SUBLANES * a:SUBLANES * a + CONV_ROWS, :]
                    acc = acc + cbw_ref[j:j + 1, lanes] * rows
        convb_buf[r0:r0 + CONV_ROWS, lanes] = acc + cbb_ref[:, lanes]

    def norm_chunk(c):
        r0 = c * CONV_ROWS
        z = _layernorm_swish(convb_buf[r0:r0 + CONV_ROWS, :], lng_ref[...], lnb_ref[...])
        preb_buf[r0:r0 + CONV_ROWS, :] = z.astype(BF16)

    out = {}

    def branch_a():
        cx = proj(1, 3)
        ua_buf[HALO_A:HALO_A + tm, :] = cx[:, :d] * cx[:, d:]
        conv_a = caw_ref[0:1, :] * ua_buf[HALO_A - 2:HALO_A - 2 + tm, :]
        conv_a = conv_a + caw_ref[1:2, :] * ua_buf[HALO_A - 1:HALO_A - 1 + tm, :]
        conv_a = conv_a + caw_ref[2:3, :] * ua_buf[HALO_A:HALO_A + tm, :]
        out["pre_a"] = (proj(0, 1) * conv_a).astype(BF16)
        na_ref[0] = ua_buf[HALO_A + tm - (K_A - 1):HALO_A + tm, :]
        ua_buf[0:HALO_A, :] = ua_buf[tm:tm + HALO_A, :]

    def branch_a_out():
        out["ya"] = gate(0) * _dot(out["pre_a"], woa_ref[...])

    def query():
        out["q"] = (proj(5, 6) * (dh ** -0.5)).astype(BF16)
        out["heads"] = []

    def head(hd):
        sl = slice(hd * dh, (hd + 1) * dh)
        s = _dot(out["q"][:, sl], kt_ref[0, sl, :])
        e = jnp.exp(s - jnp.max(s, axis=-1, keepdims=True))
        p = e * (1.0 / jnp.sum(e, axis=-1, keepdims=True))
        out["heads"].append(_dot(p.astype(BF16), vb_ref[0, :, sl]))

    def branch_x_out():
        o = jnp.concatenate(out["heads"], axis=-1).astype(BF16)
        out["yx"] = gate(2) * _dot(o, wox_ref[...])

    def gate_b():
        out["gb"] = gate(1)

    mxu_work = [branch_a, branch_a_out, query,
                functools.partial(head, 0), functools.partial(head, 1),
                functools.partial(head, 2), functools.partial(head, 3),
                branch_x_out, gate_b]
    n_chunks = tm // CONV_ROWS
    vector_work = [glu]
    for c in range(n_chunks):
        for lb in range(d // LANES):
            vector_work.append(functools.partial(conv_block, c, lb))
        vector_work.append(functools.partial(norm_chunk, c))
    done = 0
    for i, work in enumerate(vector_work):
        work()
        while done < (i + 1) * len(mxu_work) // len(vector_work):
            mxu_work[done]()
            done += 1

    nb_ref[0] = ub_buf[HALO_B + tm - (K_B - 1):HALO_B + tm, :]
    ub_buf[0:HALO_B, :] = ub_buf[tm:tm + HALO_B, :]

    yb = _dot(preb_buf[...], wob_ref[...])
    merged = out["ya"] + out["gb"] * yb + out["yx"]
    h_ref[0] = x + _dot(merged.astype(BF16), wo_ref[...])


def _mixer_call(x, kt, vb, w_in, woa, wob, wox, wo, nmix, bgate, caw, cbw, cbb, lng, lnb):
    b, t, d = x.shape
    tm = PROMPT_TILE
    n_mem = vb.shape[1]
    d_in = w_in.shape[1]
    kernel = functools.partial(_mixer_kernel, tm=tm, d=d)
    return pl.pallas_call(
        kernel,
        grid=(b, t // tm),
        in_specs=[
            pl.BlockSpec((1, tm, d), lambda i, j: (i, j, 0)),
            pl.BlockSpec((1, d, n_mem), lambda i, j: (i, 0, 0)),
            pl.BlockSpec((1, n_mem, d), lambda i, j: (i, 0, 0)),
            _resident((d, d_in)),
            _resident((d, d)), _resident((d, d)), _resident((d, d)), _resident((d, d)),
            _resident((1, d)),
            _resident((1, 3 * d)),
            _resident((K_A, d)),
            _resident((K_B, d)),
            _resident((1, d)), _resident((1, d)), _resident((1, d)),
        ],
        out_specs=[
            pl.BlockSpec((1, tm, d), lambda i, j: (i, j, 0)),
            pl.BlockSpec((1, K_A - 1, d), lambda i, j: (i, 0, 0)),
            pl.BlockSpec((1, K_B - 1, d), lambda i, j: (i, 0, 0)),
        ],
        out_shape=[
            jax.ShapeDtypeStruct((b, t, d), F32),
            jax.ShapeDtypeStruct((b, K_A - 1, d), F32),
            jax.ShapeDtypeStruct((b, K_B - 1, d), F32),
        ],
        scratch_shapes=[
            pltpu.VMEM((HALO_A + tm, d), F32),
            pltpu.VMEM((HALO_B + tm, d), F32),
            pltpu.VMEM((tm, d), F32),
            pltpu.VMEM((tm, d), BF16),
        ],
        compiler_params=pltpu.CompilerParams(
            dimension_semantics=("arbitrary", "arbitrary"), vmem_limit_bytes=62 * MIB),
        name="mixer",
    )(x, kt, vb, w_in, woa, wob, wox, wo, nmix, bgate, caw, cbw, cbb, lng, lnb)


def _token_attention(q, k, v):
    n_mem, cs, lanes = k.shape
    scale = (cs // N_HEADS * lanes) ** -0.5
    s = jnp.sum(k * (q * scale)[None], axis=-1, keepdims=True)
    s = s + pltpu.roll(s, N_HEADS, 1)
    e = jnp.exp(s - jnp.max(s, axis=0, keepdims=True))
    den = jnp.sum(e, axis=0)
    return jnp.sum(e * v, axis=0) * (1.0 / den)


def _ffn_kernel(h_ref, nffn_ref, wg_ref, wu_ref, wd_ref, nfin_ref, *rest, attn_tokens):
    if attn_tokens:
        qv_ref, k_ref, v_ref, y_ref, o_ref = rest
        for i in range(attn_tokens):
            o_ref[i] = _token_attention(qv_ref[i], k_ref[i], v_ref[i])
    else:
        (y_ref,) = rest
    h = h_ref[...]
    hn = _rms(h, nffn_ref[...]).astype(BF16)
    gate = _dot(hn, wg_ref[...])
    up = _dot(hn, wu_ref[...])
    act = (gate * _sigmoid(gate) * up).astype(BF16)
    h2 = h + _dot(act, wd_ref[...])
    y_ref[...] = _rms(h2, nfin_ref[...])


def _ffn_call(h, nffn, wg, wu, wd, nfin, tm, attn=None):
    m, d = h.shape
    dff = wg.shape[1]
    steps = m // tm
    in_specs = [
        pl.BlockSpec((tm, d), lambda i: (i, 0)),
        _resident((1, d)),
        _resident((d, dff)), _resident((d, dff)), _resident((dff, d)),
        _resident((1, d)),
    ]
    out_specs = [pl.BlockSpec((tm, d), lambda i: (i, 0))]
    out_shape = [jax.ShapeDtypeStruct((m, d), F32)]
    args = [h, nffn, wg, wu, wd, nfin]
    attn_tokens = 0
    if attn is not None:
        qv, k, v = attn
        n_tok, n_mem, cs, lanes = k.shape
        assert n_tok % steps == 0
        attn_tokens = n_tok // steps
        chunk_spec = pl.BlockSpec((attn_tokens, cs, lanes), lambda i: (i, 0, 0))
        cache_spec = pl.BlockSpec((attn_tokens, n_mem, cs, lanes), lambda i: (i, 0, 0, 0))
        in_specs += [chunk_spec, cache_spec, cache_spec]
        out_specs.append(chunk_spec)
        out_shape.append(jax.ShapeDtypeStruct((n_tok, cs, lanes), F32))
        args += [qv, k, v]
    return pl.pallas_call(
        functools.partial(_ffn_kernel, attn_tokens=attn_tokens),
        grid=(steps,),
        in_specs=in_specs,
        out_specs=out_specs,
        out_shape=out_shape,
        compiler_params=pltpu.CompilerParams(
            dimension_semantics=("arbitrary",), vmem_limit_bytes=56 * MIB),
        name="ffn_attn" if attn_tokens else "ffn",
    )(*args)


def _sproj_kernel(x_ref, nmix_ref, w_ref, o_ref, wb_ref):
    xn = _rms(x_ref[...], nmix_ref[...]).astype(BF16)
    wb = w_ref[...].astype(BF16)
    wb_ref[...] = wb
    o_ref[...] = _dot(xn, wb)


def _sproj_call(x, nmix, w_in):
    m, d = x.shape
    d_in = w_in.shape[1]
    return pl.pallas_call(
        _sproj_kernel,
        grid=(d_in // d,),
        in_specs=[
            _resident((m, d)),
            _resident((1, d)),
            pl.BlockSpec((d, d), lambda j: (0, j)),
        ],
        out_specs=[pl.BlockSpec((m, d), lambda j: (0, j)),
                   pl.BlockSpec((d, d), lambda j: (0, j))],
        out_shape=[jax.ShapeDtypeStruct((m, d_in), F32),
                   jax.ShapeDtypeStruct((d, d_in), BF16)],
        compiler_params=pltpu.CompilerParams(
            dimension_semantics=("arbitrary",), vmem_limit_bytes=32 * MIB),
        name="sample_proj",
    )(x, nmix, w_in)


def _sstate_kernel(pr_ref, sa_ref, sb_ref, caw_ref, cbw_ref, cbb_ref,
                   lng_ref, lnb_ref, prea_ref, preb_ref, na_ref, nb_ref, *, tb, d):
    a_b = pr_ref[:, 0:d]
    ua = pr_ref[:, d:2 * d] * pr_ref[:, 2 * d:3 * d]
    ub = pr_ref[:, 3 * d:4 * d] * _sigmoid(pr_ref[:, 4 * d:5 * d])

    conv_b = cbw_ref[K_B - 1:K_B, :] * ub + cbb_ref[...]
    for j in range(K_B - 1):
        conv_b = conv_b + cbw_ref[j:j + 1, :] * sb_ref[j]
    preb_ref[...] = _layernorm_swish(conv_b, lng_ref[...], lnb_ref[...])
    for j in range(K_B - 2):
        nb_ref[j] = sb_ref[j + 1]
    nb_ref[K_B - 2] = ub

    for i in range(tb):
        row = slice(i, i + 1)
        conv_a = (caw_ref[0:1, :] * sa_ref[i, 0:1, :] + caw_ref[1:2, :] * sa_ref[i, 1:2, :]
                  + caw_ref[2:3, :] * ua[row])
        prea_ref[row, :] = a_b[row] * conv_a
        na_ref[i, 0:1, :] = sa_ref[i, 1:2, :]
        na_ref[i, 1:2, :] = ua[row]


def _sstate_call(proj, sa, sb, caw, cbw, cbb, lng, lnb):
    m = proj.shape[0]
    d = sb.shape[2]
    tb = SAMPLE_TOKENS
    kernel = functools.partial(_sstate_kernel, tb=tb, d=d)
    row_spec = pl.BlockSpec((tb, d), lambda i: (i, 0))
    return pl.pallas_call(
        kernel,
        grid=(m // tb,),
        in_specs=[
            pl.BlockSpec((tb, 5 * d), lambda i: (i, 0)),
            pl.BlockSpec((tb, K_A - 1, d), lambda i: (i, 0, 0)),
            pl.BlockSpec((K_B - 1, tb, d), lambda i: (0, i, 0)),
            _resident((K_A, d)),
            _resident((K_B, d)),
            _resident((1, d)), _resident((1, d)), _resident((1, d)),
        ],
        out_specs=[
            row_spec, row_spec,
            pl.BlockSpec((tb, K_A - 1, d), lambda i: (i, 0, 0)),
            pl.BlockSpec((K_B - 1, tb, d), lambda i: (0, i, 0)),
        ],
        out_shape=[
            jax.ShapeDtypeStruct((m, d), F32),
            jax.ShapeDtypeStruct((m, d), F32),
            jax.ShapeDtypeStruct((m, K_A - 1, d), F32),
            jax.ShapeDtypeStruct((K_B - 1, m, d), F32),
        ],
        compiler_params=pltpu.CompilerParams(
            dimension_semantics=("arbitrary",), vmem_limit_bytes=32 * MIB),
        name="sample_state",
    )(proj, sa, sb, caw, cbw, cbb, lng, lnb)


def _to_chunks(a, lanes):
    *lead, nh, dh = a.shape
    a = a.reshape(*lead, nh, dh // lanes, lanes)
    a = jnp.swapaxes(a, -3, -2)
    return a.reshape(*lead, (dh // lanes) * nh, lanes)


def _from_chunks(a, nh):
    *lead, cs, lanes = a.shape
    a = a.reshape(*lead, cs // nh, nh, lanes)
    a = jnp.swapaxes(a, -3, -2)
    return a.reshape(*lead, nh, (cs // nh) * lanes)


def _spost_kernel(x_ref, prea_ref, preb_ref, o_ref, gl_ref, bgate_ref,
                  woa_ref, wob_ref, wox_ref, wo_ref, h_ref, *, d):
    ya = _dot(prea_ref[...].astype(BF16), woa_ref[...])
    yb = _dot(preb_ref[...].astype(BF16), wob_ref[...])
    yx = _dot(o_ref[...].astype(BF16), wox_ref[...])
    g = _sigmoid(gl_ref[...] + bgate_ref[...])
    merged = g[:, :d] * ya + g[:, d:2 * d] * yb + g[:, 2 * d:] * yx
    h_ref[...] = x_ref[...] + _dot(merged.astype(BF16), wo_ref[...])


def _spost_call(x, prea, preb, o, proj, bgate, woa, wob, wox, wo):
    m, d = x.shape
    kernel = functools.partial(_spost_kernel, d=d)
    full = pl.BlockSpec((m, d), lambda i: (0, 0))
    return pl.pallas_call(
        kernel,
        grid=(1,),
        in_specs=[
            full, full, full, full,
            pl.BlockSpec((m, 3 * d), lambda i: (0, 2)),
            _resident((1, 3 * d)),
            _resident((d, d)), _resident((d, d)), _resident((d, d)), _resident((d, d)),
        ],
        out_specs=full,
        out_shape=jax.ShapeDtypeStruct((m, d), F32),
        compiler_params=pltpu.CompilerParams(
            dimension_semantics=("arbitrary",), vmem_limit_bytes=32 * MIB),
        name="sample_post",
    )(x, prea, preb, o, proj, bgate, woa, wob, wox, wo)


def kernel(x_prompt, x_sample, mem_prompt, cache_mem_k, cache_mem_v, state_conv_a, state_conv_b, norm_mix, w_in, b_gate, conv_a_w, w_out_a, conv_b_w, conv_b_bias, ln_b_g, ln_b_b, w_out_b, norm_mem, w_k, w_v, w_out_x, w_o, norm_ffn, w_ff_gate, w_ff_up, w_ff_down, norm_final):
    depth = w_in.shape[0]
    assert depth == 1, "single-layer step only"
    b, t, d = x_prompt.shape
    sb_, st_, _ = x_sample.shape
    assert st_ == 1
    n_mem = mem_prompt.shape[1]
    nh, dh = w_k.shape[2], w_k.shape[3]
    assert nh == N_HEADS and nh * dh == d
    assert dh == 2 * LANES and 2 * nh == SUBLANES

    row = lambda v: v.reshape(1, -1)
    wk = w_k[0].reshape(d, d).astype(BF16)
    wv = w_v[0].reshape(d, d).astype(BF16)
    nmix, nffn, nfin, nmem = row(norm_mix[0]), row(norm_ffn[0]), row(norm_final), row(norm_mem[0])
    bgate, cbb, lng, lnb = row(b_gate[0]), row(conv_b_bias[0]), row(ln_b_g[0]), row(ln_b_b[0])
    caw, cbw = conv_a_w[0], conv_b_w[0]

    (k_p, v_p, kt, vb), (woa, wob, wox, wo, wg, wu, wd) = _kv_call(
        mem_prompt, nmem, wk, wv,
        [w_out_a[0], w_out_b[0], w_out_x[0], w_o[0], w_ff_gate[0], w_ff_up[0], w_ff_down[0]])

    xs = x_sample.reshape(sb_, d)
    proj_s, w_in_b = _sproj_call(xs, nmix, w_in[0])

    h_p, ca_p, cb_p = _mixer_call(x_prompt, kt, vb, w_in_b, woa, wob, wox, wo,
                                  nmix, bgate, caw, cbw, cbb, lng, lnb)

    q_s = _to_chunks(proj_s[:, 5 * d:6 * d].reshape(sb_, nh, dh), LANES)
    prea, preb, ca_s, cb_s = _sstate_call(
        proj_s, state_conv_a[0], jnp.swapaxes(state_conv_b[0], 0, 1), caw, cbw, cbb, lng, lnb)

    y_p, o_s = _ffn_call(
        h_p.reshape(b * t, d), nffn, wg, wu, wd, nfin, PROMPT_TILE,
        attn=(q_s, _to_chunks(cache_mem_k[0], LANES), _to_chunks(cache_mem_v[0], LANES)))

    h_s = _spost_call(xs, prea, preb, _from_chunks(o_s, nh).reshape(sb_, d), proj_s,
                      bgate, woa, wob, wox, wo)
    (y_s,) = _ffn_call(h_s, nffn, wg, wu, wd, nfin, sb_)

    return (y_p.reshape(b, t, d), y_s.reshape(sb_, 1, d),
            _from_chunks(k_p, nh)[None], _from_chunks(v_p, nh)[None],
            ca_p[None], cb_p[None], ca_s[None], jnp.swapaxes(cb_s, 0, 1)[None])
```

```python
import functools

import jax
import jax.numpy as jnp
from jax import lax
from jax.experimental import pallas as pl
from jax.experimental.pallas import tpu as pltpu

F32 = jnp.float32
BF16 = jnp.bfloat16
EPS = 1e-6
LOG2_E = 1.4426950408889634

K_A = 3
K_B = 31
N_HEADS = 4
SUBLANES = 8
LANES = 128
HALO_A = 8
HALO_B = 32
CONV_ROWS = 64

PROMPT_TILE = 512
SAMPLE_TOKENS = 32
MIB = 1024 * 1024


def _dot(a, b):
    return jnp.dot(a, b, preferred_element_type=F32)


def _rms(x, g):
    ms = jnp.mean(x * x, axis=-1, keepdims=True)
    return x * lax.rsqrt(ms + EPS) * g


def _sigmoid(x):
    return 1.0 / (1.0 + jnp.exp2(x * (-LOG2_E)))


def _layernorm_swish(y, g, b):
    mu = jnp.mean(y, axis=-1, keepdims=True)
    yc = y - mu
    var = jnp.mean(yc * yc, axis=-1, keepdims=True)
    z = yc * lax.rsqrt(var + EPS) * g + b
    return z * _sigmoid(z)


def _resident(shape):
    nd = len(shape)
    return pl.BlockSpec(shape, lambda *_: (0,) * nd, pipeline_mode=pl.Buffered(1))


def _kv_kernel(mem_ref, g_ref, wk_ref, wv_ref, *rest, n_cast):
    cast_in, (k_ref, v_ref, kt_ref, vb_ref), cast_out = (
        rest[:n_cast], rest[n_cast:n_cast + 4], rest[n_cast + 4:])
    for src, dst in zip(cast_in, cast_out):
        dst[...] = src[...].astype(BF16)
    mn = _rms(mem_ref[0], g_ref[...]).astype(BF16)
    k = _dot(mn, wk_ref[...])
    v = _dot(mn, wv_ref[...])
    vb_ref[0] = v.astype(BF16)
    kt_ref[0] = k.T.astype(BF16)
    cs, lanes = k_ref.shape[2:]
    halves = cs // N_HEADS
    dh = halves * lanes
    for half in range(halves):
        for hd in range(N_HEADS):
            cols = slice(hd * dh + half * lanes, hd * dh + (half + 1) * lanes)
            k_ref[0, :, half * N_HEADS + hd, :] = k[:, cols]
            v_ref[0, :, half * N_HEADS + hd, :] = v[:, cols]


def _kv_call(mem, g, wk, wv, cast_weights):
    b, n_mem, d = mem.shape
    dx = wk.shape[1]
    cs = dx // LANES
    chunk_spec = pl.BlockSpec((1, n_mem, cs, LANES), lambda i: (i, 0, 0, 0))
    bf16_rows = 2 * SUBLANES
    cast_specs = []
    for w in cast_weights:
        rows, cols = w.shape
        assert rows % (b * bf16_rows) == 0
        cast_specs.append(pl.BlockSpec((rows // b, cols), lambda i: (i, 0)))
    outs = pl.pallas_call(
        functools.partial(_kv_kernel, n_cast=len(cast_weights)),
        grid=(b,),
        in_specs=[
            pl.BlockSpec((1, n_mem, d), lambda i: (i, 0, 0)),
            _resident((1, d)),
            _resident((d, dx)),
            _resident((d, dx)),
        ] + cast_specs,
        out_specs=[
            chunk_spec, chunk_spec,
            pl.BlockSpec((1, dx, n_mem), lambda i: (i, 0, 0)),
            pl.BlockSpec((1, n_mem, dx), lambda i: (i, 0, 0)),
        ] + cast_specs,
        out_shape=[
            jax.ShapeDtypeStruct((b, n_mem, cs, LANES), F32),
            jax.ShapeDtypeStruct((b, n_mem, cs, LANES), F32),
            jax.ShapeDtypeStruct((b, dx, n_mem), BF16),
            jax.ShapeDtypeStruct((b, n_mem, dx), BF16),
        ] + [jax.ShapeDtypeStruct(w.shape, BF16) for w in cast_weights],
        compiler_params=pltpu.CompilerParams(
            dimension_semantics=("arbitrary",), vmem_limit_bytes=48 * MIB),
        name="kv",
    )(mem, g, wk, wv, *cast_weights)
    return outs[:4], outs[4:]


def _mixer_kernel(x_ref, kt_ref, vb_ref, w_in_ref, woa_ref, wob_ref, wox_ref, wo_ref,
                  nmix_ref, bgate_ref, caw_ref, cbw_ref, cbb_ref, lng_ref, lnb_ref, *rest,
                  tm, d, n_cast):
    cast_in, (h_ref, na_ref, nb_ref), cast_out, scratch = (
        rest[:n_cast], rest[n_cast:n_cast + 3], rest[n_cast + 3:2 * n_cast + 3],
        rest[2 * n_cast + 3:])
    ua_buf, ub_buf, convb_buf, preb_buf = scratch
    t = pl.program_id(1)
    dh = d // N_HEADS

    for src, dst in zip(cast_in, cast_out):
        dst[...] = src[...].astype(BF16)

    @pl.when(t == 0)
    def _():
        ua_buf[0:HALO_A, :] = jnp.zeros((HALO_A, d), F32)
        ub_buf[0:HALO_B, :] = jnp.zeros((HALO_B, d), F32)

    x = x_ref[0]
    xn = _rms(x, nmix_ref[...]).astype(BF16)

    def proj(c0, c1):
        return _dot(xn, w_in_ref[:, c0 * d:c1 * d])

    def gate(i):
        return _sigmoid(proj(6 + i, 7 + i) + bgate_ref[:, i * d:(i + 1) * d])

    def glu():
        ub_buf[HALO_B:HALO_B + tm, :] = proj(3, 4) * _sigmoid(proj(4, 5))

    def conv_block(c, lb):
        r0 = c * CONV_ROWS
        win_rows = CONV_ROWS + HALO_B
        lanes = slice(lb * LANES, (lb + 1) * LANES)
        win = ub_buf[r0:r0 + win_rows, lanes]
        acc = jnp.zeros((CONV_ROWS, LANES), F32)
        for r in range(SUBLANES):
            rolled = win if r == 0 else pltpu.roll(win, win_rows - r, 0)
            for a in range(win_rows // SUBLANES):
                j = SUBLANES * a + r - (HALO_B - (K_B - 1))
                if 0 <= j < K_B:
                    rows = rolled[SUBLANES * a:SUBLANES * a + CONV_ROWS, :]
                    acc = acc + cbw_ref[j:j + 1, lanes] * rows
        convb_buf[r0:r0 + CONV_ROWS, lanes] = acc + cbb_ref[:, lanes]

    def norm_chunk(c):
        r0 = c * CONV_ROWS
        z = _layernorm_swish(convb_buf[r0:r0 + CONV_ROWS, :], lng_ref[...], lnb_ref[...])
        preb_buf[r0:r0 + CONV_ROWS, :] = z.astype(BF16)

    out = {}

    def branch_a():
        cx = proj(1, 3)
        ua_buf[HALO_A:HALO_A + tm, :] = cx[:, :d] * cx[:, d:]
        conv_a = caw_ref[0:1, :] * ua_buf[HALO_A - 2:HALO_A - 2 + tm, :]
        conv_a = conv_a + caw_ref[1:2, :] * ua_buf[HALO_A - 1:HALO_A - 1 + tm, :]
        conv_a = conv_a + caw_ref[2:3, :] * ua_buf[HALO_A:HALO_A + tm, :]
        out["pre_a"] = (proj(0, 1) * conv_a).astype(BF16)
        na_ref[0] = ua_buf[HALO_A + tm - (K_A - 1):HALO_A + tm, :]
        ua_buf[0:HALO_A, :] = ua_buf[tm:tm + HALO_A, :]

    def branch_a_out():
        out["ya"] = gate(0) * _dot(out["pre_a"], woa_ref[...])

    def query():
        out["q"] = (proj(5, 6) * (dh ** -0.5)).astype(BF16)
        out["heads"] = []

    def head(hd):
        sl = slice(hd * dh, (hd + 1) * dh)
        s = _dot(out["q"][:, sl], kt_ref[0, sl, :])
        e = jnp.exp(s - jnp.max(s, axis=-1, keepdims=True))
        p = e * (1.0 / jnp.sum(e, axis=-1, keepdims=True))
        out["heads"].append(_dot(p.astype(BF16), vb_ref[0, :, sl]))

    def branch_x_out():
        o = jnp.concatenate(out["heads"], axis=-1).astype(BF16)
        out["yx"] = gate(2) * _dot(o, wox_ref[...])

    def gate_b():
        out["gb"] = gate(1)

    mxu_work = [branch_a, branch_a_out, query,
                functools.partial(head, 0), functools.partial(head, 1),
                functools.partial(head, 2), functools.partial(head, 3),
                branch_x_out, gate_b]
    n_chunks = tm // CONV_ROWS
    vector_work = [glu]
    for c in range(n_chunks):
        for lb in range(d // LANES):
            vector_work.append(functools.partial(conv_block, c, lb))
        vector_work.append(functools.partial(norm_chunk, c))
    done = 0
    for i, work in enumerate(vector_work):
        work()
        while done < (i + 1) * len(mxu_work) // len(vector_work):
            mxu_work[done]()
            done += 1

    nb_ref[0] = ub_buf[HALO_B + tm - (K_B - 1):HALO_B + tm, :]
    ub_buf[0:HALO_B, :] = ub_buf[tm:tm + HALO_B, :]

    yb = _dot(preb_buf[...], wob_ref[...])
    merged = out["ya"] + out["gb"] * yb + out["yx"]
    h_ref[0] = x + _dot(merged.astype(BF16), wo_ref[...])


def _mixer_call(x, kt, vb, w_in, woa, wob, wox, wo, nmix, bgate, caw, cbw, cbb, lng, lnb,
                cast_weights):
    b, t, d = x.shape
    tm = PROMPT_TILE
    nt = t // tm
    steps = b * nt
    n_mem = vb.shape[1]
    d_in = w_in.shape[1]
    bf16_rows = 2 * SUBLANES
    cast_specs = []
    for w in cast_weights:
        rows, cols = w.shape
        chunks = steps
        while rows % (chunks * bf16_rows):
            chunks //= 2
        cast_specs.append(pl.BlockSpec(
            (rows // chunks, cols),
            lambda i, j, chunks=chunks: (jnp.minimum(i * nt + j, chunks - 1), 0)))

    kernel = functools.partial(_mixer_kernel, tm=tm, d=d, n_cast=len(cast_weights))
    outs = pl.pallas_call(
        kernel,
        grid=(b, nt),
        in_specs=[
            pl.BlockSpec((1, tm, d), lambda i, j: (i, j, 0)),
            pl.BlockSpec((1, d, n_mem), lambda i, j: (i, 0, 0)),
            pl.BlockSpec((1, n_mem, d), lambda i, j: (i, 0, 0)),
            _resident((d, d_in)),
            _resident((d, d)), _resident((d, d)), _resident((d, d)), _resident((d, d)),
            _resident((1, d)),
            _resident((1, 3 * d)),
            _resident((K_A, d)),
            _resident((K_B, d)),
            _resident((1, d)), _resident((1, d)), _resident((1, d)),
        ] + cast_specs,
        out_specs=[
            pl.BlockSpec((1, tm, d), lambda i, j: (i, j, 0)),
            pl.BlockSpec((1, K_A - 1, d), lambda i, j: (i, 0, 0)),
            pl.BlockSpec((1, K_B - 1, d), lambda i, j: (i, 0, 0)),
        ] + cast_specs,
        out_shape=[
            jax.ShapeDtypeStruct((b, t, d), F32),
            jax.ShapeDtypeStruct((b, K_A - 1, d), F32),
            jax.ShapeDtypeStruct((b, K_B - 1, d), F32),
        ] + [jax.ShapeDtypeStruct(w.shape, BF16) for w in cast_weights],
        scratch_shapes=[
            pltpu.VMEM((HALO_A + tm, d), F32),
            pltpu.VMEM((HALO_B + tm, d), F32),
            pltpu.VMEM((tm, d), F32),
            pltpu.VMEM((tm, d), BF16),
        ],
        compiler_params=pltpu.CompilerParams(
            dimension_semantics=("arbitrary", "arbitrary"), vmem_limit_bytes=62 * MIB),
        name="mixer",
    )(x, kt, vb, w_in, woa, wob, wox, wo, nmix, bgate, caw, cbw, cbb, lng, lnb,
      *cast_weights)
    return outs[:3], outs[3:]


def _token_attention(q, k, v):
    n_mem, cs, lanes = k.shape
    scale = (cs // N_HEADS * lanes) ** -0.5
    s = jnp.sum(k * (q * scale)[None], axis=-1, keepdims=True)
    s = s + pltpu.roll(s, N_HEADS, 1)
    e = jnp.exp(s - jnp.max(s, axis=0, keepdims=True))
    den = jnp.sum(e, axis=0)
    return jnp.sum(e * v, axis=0) * (1.0 / den)


def _ffn_kernel(h_ref, nffn_ref, wg_ref, wu_ref, wd_ref, nfin_ref, *rest, attn_tokens):
    if attn_tokens:
        qv_ref, k_ref, v_ref, y_ref, o_ref = rest
        for i in range(attn_tokens):
            o_ref[i] = _token_attention(qv_ref[i], k_ref[i], v_ref[i])
    else:
        (y_ref,) = rest
    h = h_ref[...]
    hn = _rms(h, nffn_ref[...]).astype(BF16)
    gate = _dot(hn, wg_ref[...])
    up = _dot(hn, wu_ref[...])
    act = (gate * _sigmoid(gate) * up).astype(BF16)
    h2 = h + _dot(act, wd_ref[...])
    y_ref[...] = _rms(h2, nfin_ref[...])


def _ffn_call(h, nffn, wg, wu, wd, nfin, tm, attn=None):
    m, d = h.shape
    dff = wg.shape[1]
    steps = m // tm
    in_specs = [
        pl.BlockSpec((tm, d), lambda i: (i, 0)),
        _resident((1, d)),
        _resident((d, dff)), _resident((d, dff)), _resident((dff, d)),
        _resident((1, d)),
    ]
    out_specs = [pl.BlockSpec((tm, d), lambda i: (i, 0))]
    out_shape = [jax.ShapeDtypeStruct((m, d), F32)]
    args = [h, nffn, wg, wu, wd, nfin]
    attn_tokens = 0
    if attn is not None:
        qv, k, v = attn
        n_tok, n_mem, cs, lanes = k.shape
        assert n_tok % steps == 0
        attn_tokens = n_tok // steps
        chunk_spec = pl.BlockSpec((attn_tokens, cs, lanes), lambda i: (i, 0, 0))
        cache_spec = pl.BlockSpec((attn_tokens, n_mem, cs, lanes), lambda i: (i, 0, 0, 0))
        in_specs += [chunk_spec, cache_spec, cache_spec]
        out_specs.append(chunk_spec)
        out_shape.append(jax.ShapeDtypeStruct((n_tok, cs, lanes), F32))
        args += [qv, k, v]
    return pl.pallas_call(
        functools.partial(_ffn_kernel, attn_tokens=attn_tokens),
        grid=(steps,),
        in_specs=in_specs,
        out_specs=out_specs,
        out_shape=out_shape,
        compiler_params=pltpu.CompilerParams(
            dimension_semantics=("arbitrary",), vmem_limit_bytes=56 * MIB),
        name="ffn_attn" if attn_tokens else "ffn",
    )(*args)


def _sproj_kernel(x_ref, nmix_ref, w_ref, o_ref, wb_ref):
    xn = _rms(x_ref[...], nmix_ref[...]).astype(BF16)
    wb = w_ref[...].astype(BF16)
    wb_ref[...] = wb
    o_ref[...] = _dot(xn, wb)


def _sproj_call(x, nmix, w_in):
    m, d = x.shape
    d_in = w_in.shape[1]
    return pl.pallas_call(
        _sproj_kernel,
        grid=(d_in // d,),
        in_specs=[
            _resident((m, d)),
            _resident((1, d)),
            pl.BlockSpec((d, d), lambda j: (0, j)),
        ],
        out_specs=[pl.BlockSpec((m, d), lambda j: (0, j)),
                   pl.BlockSpec((d, d), lambda j: (0, j))],
        out_shape=[jax.ShapeDtypeStruct((m, d_in), F32),
                   jax.ShapeDtypeStruct((d, d_in), BF16)],
        compiler_params=pltpu.CompilerParams(
            dimension_semantics=("arbitrary",), vmem_limit_bytes=32 * MIB),
        name="sample_proj",
    )(x, nmix, w_in)


def _sstate_kernel(pr_ref, sa_ref, sb_ref, caw_ref, cbw_ref, cbb_ref,
                   lng_ref, lnb_ref, prea_ref, preb_ref, na_ref, nb_ref, *, tb, d):
    a_b = pr_ref[:, 0:d]
    ua = pr_ref[:, d:2 * d] * pr_ref[:, 2 * d:3 * d]
    ub = pr_ref[:, 3 * d:4 * d] * _sigmoid(pr_ref[:, 4 * d:5 * d])

    conv_b = cbw_ref[K_B - 1:K_B, :] * ub + cbb_ref[...]
    for j in range(K_B - 1):
        conv_b = conv_b + cbw_ref[j:j + 1, :] * sb_ref[j]
    preb_ref[...] = _layernorm_swish(conv_b, lng_ref[...], lnb_ref[...])
    for j in range(K_B - 2):
        nb_ref[j] = sb_ref[j + 1]
    nb_ref[K_B - 2] = ub

    for i in range(tb):
        row = slice(i, i + 1)
        conv_a = (caw_ref[0:1, :] * sa_ref[i, 0:1, :] + caw_ref[1:2, :] * sa_ref[i, 1:2, :]
                  + caw_ref[2:3, :] * ua[row])
        prea_ref[row, :] = a_b[row] * conv_a
        na_ref[i, 0:1, :] = sa_ref[i, 1:2, :]
        na_ref[i, 1:2, :] = ua[row]


def _sstate_call(proj, sa, sb, caw, cbw, cbb, lng, lnb):
    m = proj.shape[0]
    d = sb.shape[2]
    tb = SAMPLE_TOKENS
    kernel = functools.partial(_sstate_kernel, tb=tb, d=d)
    row_spec = pl.BlockSpec((tb, d), lambda i: (i, 0))
    return pl.pallas_call(
        kernel,
        grid=(m // tb,),
        in_specs=[
            pl.BlockSpec((tb, 5 * d), lambda i: (i, 0)),
            pl.BlockSpec((tb, K_A - 1, d), lambda i: (i, 0, 0)),
            pl.BlockSpec((K_B - 1, tb, d), lambda i: (0, i, 0)),
            _resident((K_A, d)),
            _resident((K_B, d)),
            _resident((1, d)), _resident((1, d)), _resident((1, d)),
        ],
        out_specs=[
            row_spec, row_spec,
            pl.BlockSpec((tb, K_A - 1, d), lambda i: (i, 0, 0)),
            pl.BlockSpec((K_B - 1, tb, d), lambda i: (0, i, 0)),
        ],
        out_shape=[
            jax.ShapeDtypeStruct((m, d), F32),
            jax.ShapeDtypeStruct((m, d), F32),
            jax.ShapeDtypeStruct((m, K_A - 1, d), F32),
            jax.ShapeDtypeStruct((K_B - 1, m, d), F32),
        ],
        compiler_params=pltpu.CompilerParams(
            dimension_semantics=("arbitrary",), vmem_limit_bytes=32 * MIB),
        name="sample_state",
    )(proj, sa, sb, caw, cbw, cbb, lng, lnb)


def _to_chunks(a, lanes):
    *lead, nh, dh = a.shape
    a = a.reshape(*lead, nh, dh // lanes, lanes)
    a = jnp.swapaxes(a, -3, -2)
    return a.reshape(*lead, (dh // lanes) * nh, lanes)


def _from_chunks(a, nh):
    *lead, cs, lanes = a.shape
    a = a.reshape(*lead, cs // nh, nh, lanes)
    a = jnp.swapaxes(a, -3, -2)
    return a.reshape(*lead, nh, (cs // nh) * lanes)


def _spost_kernel(x_ref, prea_ref, preb_ref, o_ref, gl_ref, bgate_ref,
                  woa_ref, wob_ref, wox_ref, wo_ref, h_ref, *, d):
    ya = _dot(prea_ref[...].astype(BF16), woa_ref[...])
    yb = _dot(preb_ref[...].astype(BF16), wob_ref[...])
    yx = _dot(o_ref[...].astype(BF16), wox_ref[...])
    g = _sigmoid(gl_ref[...] + bgate_ref[...])
    merged = g[:, :d] * ya + g[:, d:2 * d] * yb + g[:, 2 * d:] * yx
    h_ref[...] = x_ref[...] + _dot(merged.astype(BF16), wo_ref[...])


def _spost_call(x, prea, preb, o, proj, bgate, woa, wob, wox, wo):
    m, d = x.shape
    kernel = functools.partial(_spost_kernel, d=d)
    full = pl.BlockSpec((m, d), lambda i: (0, 0))
    return pl.pallas_call(
        kernel,
        grid=(1,),
        in_specs=[
            full, full, full, full,
            pl.BlockSpec((m, 3 * d), lambda i: (0, 2)),
            _resident((1, 3 * d)),
            _resident((d, d)), _resident((d, d)), _resident((d, d)), _resident((d, d)),
        ],
        out_specs=full,
        out_shape=jax.ShapeDtypeStruct((m, d), F32),
        compiler_params=pltpu.CompilerParams(
            dimension_semantics=("arbitrary",), vmem_limit_bytes=32 * MIB),
        name="sample_post",
    )(x, prea, preb, o, proj, bgate, woa, wob, wox, wo)


def kernel(x_prompt, x_sample, mem_prompt, cache_mem_k, cache_mem_v, state_conv_a, state_conv_b, norm_mix, w_in, b_gate, conv_a_w, w_out_a, conv_b_w, conv_b_bias, ln_b_g, ln_b_b, w_out_b, norm_mem, w_k, w_v, w_out_x, w_o, norm_ffn, w_ff_gate, w_ff_up, w_ff_down, norm_final):
    depth = w_in.shape[0]
    assert depth == 1, "single-layer step only"
    b, t, d = x_prompt.shape
    sb_, st_, _ = x_sample.shape
    assert st_ == 1
    n_mem = mem_prompt.shape[1]
    nh, dh = w_k.shape[2], w_k.shape[3]
    assert nh == N_HEADS and nh * dh == d
    assert dh == 2 * LANES and 2 * nh == SUBLANES

    row = lambda v: v.reshape(1, -1)
    wk = w_k[0].reshape(d, d).astype(BF16)
    wv = w_v[0].reshape(d, d).astype(BF16)
    nmix, nffn, nfin, nmem = row(norm_mix[0]), row(norm_ffn[0]), row(norm_final), row(norm_mem[0])
    bgate, cbb, lng, lnb = row(b_gate[0]), row(conv_b_bias[0]), row(ln_b_g[0]), row(ln_b_b[0])
    caw, cbw = conv_a_w[0], conv_b_w[0]

    (k_p, v_p, kt, vb), (woa, wob, wox, wo) = _kv_call(
        mem_prompt, nmem, wk, wv, [w_out_a[0], w_out_b[0], w_out_x[0], w_o[0]])

    xs = x_sample.reshape(sb_, d)
    proj_s, w_in_b = _sproj_call(xs, nmix, w_in[0])

    (h_p, ca_p, cb_p), (wg, wu, wd) = _mixer_call(
        x_prompt, kt, vb, w_in_b, woa, wob, wox, wo, nmix, bgate, caw, cbw, cbb, lng, lnb,
        [w_ff_gate[0], w_ff_up[0], w_ff_down[0]])

    q_s = _to_chunks(proj_s[:, 5 * d:6 * d].reshape(sb_, nh, dh), LANES)
    prea, preb, ca_s, cb_s = _sstate_call(
        proj_s, state_conv_a[0], jnp.swapaxes(state_conv_b[0], 0, 1), caw, cbw, cbb, lng, lnb)

    y_p, o_s = _ffn_call(
        h_p.reshape(b * t, d), nffn, wg, wu, wd, nfin, PROMPT_TILE,
        attn=(q_s, _to_chunks(cache_mem_k[0], LANES), _to_chunks(cache_mem_v[0], LANES)))

    h_s = _spost_call(xs, prea, preb, _from_chunks(o_s, nh).reshape(sb_, d), proj_s,
                      bgate, woa, wob, wox, wo)
    (y_s,) = _ffn_call(h_s, nffn, wg, wu, wd, nfin, sb_)

    return (y_p.reshape(b, t, d), y_s.reshape(sb_, 1, d),
            _from_chunks(k_p, nh)[None], _from_chunks(v_p, nh)[None],
            ca_p[None], cb_p[None], ca_s[None], jnp.swapaxes(cb_s, 0, 1)[None])
```

```python
import functools

import jax
import jax.numpy as jnp
from jax import lax
from jax.experimental import pallas as pl
from jax.experimental.pallas import tpu as pltpu

F32 = jnp.float32
BF16 = jnp.bfloat16
EPS = 1e-6
LOG2_E = 1.4426950408889634

K_A = 3
K_B = 31
N_HEADS = 4
SUBLANES = 8
LANES = 128
HALO_A = 8
HALO_B = 32
CONV_ROWS = 64

PROMPT_TILE = 512
SAMPLE_TOKENS = 8
MIB = 1024 * 1024


def _dot(a, b):
    return jnp.dot(a, b, preferred_element_type=F32)


def _rms(x, g):
    ms = jnp.mean(x * x, axis=-1, keepdims=True)
    return x * lax.rsqrt(ms + EPS) * g


def _sigmoid(x):
    return 1.0 / (1.0 + jnp.exp2(x * (-LOG2_E)))


def _layernorm_swish(y, g, b):
    mu = jnp.mean(y, axis=-1, keepdims=True)
    yc = y - mu
    var = jnp.mean(yc * yc, axis=-1, keepdims=True)
    z = yc * lax.rsqrt(var + EPS) * g + b
    return z * _sigmoid(z)


VEC_ROWS = {"nmix": (0, 1), "nmem": (1, 1), "nffn": (2, 1), "nfin": (3, 1), "cbb": (4, 1),
            "lng": (5, 1), "lnb": (6, 1), "bgate": (8, 3), "caw": (11, K_A), "cbw": (16, K_B)}
VEC_TABLE_ROWS = 48


def _pack_vectors(d, **named):
    assert set(named) == set(VEC_ROWS)
    pieces, pos = [], 0
    for name, (first, rows) in sorted(VEC_ROWS.items(), key=lambda kv: kv[1]):
        pieces += [jnp.zeros((first - pos, d), F32), named[name].reshape(rows, d)]
        pos = first + rows
    pieces.append(jnp.zeros((VEC_TABLE_ROWS - pos, d), F32))
    return jnp.concatenate([p for p in pieces if p.shape[0]], axis=0)


class _vec:
    def __init__(self, vec_ref, name):
        self.ref = vec_ref
        self.first, self.rows = VEC_ROWS[name]

    def __getitem__(self, idx):
        rows, cols = (slice(None), slice(None)) if idx is Ellipsis else idx
        start = rows.start or 0
        stop = self.rows if rows.stop is None else rows.stop
        return self.ref[self.first + start:self.first + stop, cols]


def _resident(shape):
    nd = len(shape)
    return pl.BlockSpec(shape, lambda *_: (0,) * nd, pipeline_mode=pl.Buffered(1))


def _to_tiles(a, lanes):
    rows, dx = a.shape
    dh = dx // N_HEADS
    blocks = [a[:, hd * dh + half * lanes:hd * dh + (half + 1) * lanes]
              for half in range(dh // lanes) for hd in range(N_HEADS)]
    return jnp.concatenate(blocks, axis=-1).reshape(rows, dx // lanes, lanes)


def _prep_kernel(mem_ref, vec_ref, wk_ref, wv_ref, xs_ref, win_ref, *rest,
                 n_cast, n_batch, q_block):
    cast_in = rest[:n_cast]
    k_ref, v_ref, kt_ref, vb_ref, proj_ref, winb_ref, q_ref = rest[n_cast:n_cast + 7]
    cast_out = rest[n_cast + 7:]
    g_ref, nmix_ref = _vec(vec_ref, "nmem"), _vec(vec_ref, "nmix")
    j = pl.program_id(0)
    lanes = q_ref.shape[-1]

    xn = _rms(xs_ref[...], nmix_ref[...]).astype(BF16)
    wb = win_ref[...].astype(BF16)
    winb_ref[...] = wb
    proj = _dot(xn, wb)
    proj_ref[...] = proj

    @pl.when(j == q_block)
    def _():
        q_ref[...] = _to_tiles(proj, lanes)

    @pl.when(j < n_batch)
    def _():
        for src, dst in zip(cast_in, cast_out):
            dst[...] = src[...].astype(BF16)
        mn = _rms(mem_ref[0], g_ref[...]).astype(BF16)
        k = _dot(mn, wk_ref[...])
        v = _dot(mn, wv_ref[...])
        vb_ref[0] = v.astype(BF16)
        kt_ref[0] = k.T.astype(BF16)
        k_ref[0] = _to_tiles(k, lanes)
        v_ref[0] = _to_tiles(v, lanes)


def _prep_call(mem, vecs, wk, wv, xs, w_in, cast_weights, q_block):
    b, n_mem, d = mem.shape
    m = xs.shape[0]
    dx = wk.shape[1]
    d_in = w_in.shape[1]
    steps = d_in // d
    assert b <= steps
    cs = dx // LANES
    batch = lambda j: jnp.minimum(j, b - 1)
    chunk_spec = pl.BlockSpec((1, n_mem, cs, LANES), lambda j: (batch(j), 0, 0, 0))
    bf16_rows = 2 * SUBLANES
    cast_specs = []
    for w in cast_weights:
        rows, cols = w.shape
        assert rows % (b * bf16_rows) == 0
        cast_specs.append(pl.BlockSpec((rows // b, cols), lambda j: (batch(j), 0)))
    outs = pl.pallas_call(
        functools.partial(_prep_kernel, n_cast=len(cast_weights), n_batch=b, q_block=q_block),
        grid=(steps,),
        in_specs=[
            pl.BlockSpec((1, n_mem, d), lambda j: (batch(j), 0, 0)),
            _resident(vecs.shape),
            _resident((d, dx)),
            _resident((d, dx)),
            _resident((m, d)),
            pl.BlockSpec((d, d), lambda j: (0, j)),
        ] + cast_specs,
        out_specs=[
            chunk_spec, chunk_spec,
            pl.BlockSpec((1, dx, n_mem), lambda j: (batch(j), 0, 0)),
            pl.BlockSpec((1, n_mem, dx), lambda j: (batch(j), 0, 0)),
            pl.BlockSpec((m, d), lambda j: (0, j)),
            pl.BlockSpec((d, d), lambda j: (0, j)),
            pl.BlockSpec((m, cs, LANES), lambda j: (0, 0, 0)),
        ] + cast_specs,
        out_shape=[
            jax.ShapeDtypeStruct((b, n_mem, cs, LANES), F32),
            jax.ShapeDtypeStruct((b, n_mem, cs, LANES), F32),
            jax.ShapeDtypeStruct((b, dx, n_mem), BF16),
            jax.ShapeDtypeStruct((b, n_mem, dx), BF16),
            jax.ShapeDtypeStruct((m, d_in), F32),
            jax.ShapeDtypeStruct((d, d_in), BF16),
            jax.ShapeDtypeStruct((m, cs, LANES), F32),
        ] + [jax.ShapeDtypeStruct(w.shape, BF16) for w in cast_weights],
        compiler_params=pltpu.CompilerParams(
            dimension_semantics=("arbitrary",), vmem_limit_bytes=48 * MIB),
        name="prep",
    )(mem, vecs, wk, wv, xs, w_in, *cast_weights)
    return outs[:4], outs[4:7], outs[7:]


def _mixer_kernel(x_ref, kt_ref, vb_ref, w_in_ref, woa_ref, wob_ref, wox_ref, wo_ref,
                  vec_ref, *rest, tm, d, n_cast, state_tokens):
    nmix_ref, bgate_ref, caw_ref, cbw_ref, cbb_ref, lng_ref, lnb_ref = (
        _vec(vec_ref, name) for name in ("nmix", "bgate", "caw", "cbw", "cbb", "lng", "lnb"))
    rest = list(rest)
    cast_in = [rest.pop(0) for _ in range(n_cast)]
    state_in = [rest.pop(0) for _ in range(3)]
    h_ref, na_ref, nb_ref = [rest.pop(0) for _ in range(3)]
    cast_out = [rest.pop(0) for _ in range(n_cast)]
    state_out = [rest.pop(0) for _ in range(4)]
    ua_buf, ub_buf, convb_buf, preb_buf = rest
    t = pl.program_id(1)
    dh = d // N_HEADS

    @pl.when(t == 0)
    def _():
        ua_buf[0:HALO_A, :] = jnp.zeros((HALO_A, d), F32)
        ub_buf[0:HALO_B, :] = jnp.zeros((HALO_B, d), F32)

    x = x_ref[0]
    xn = _rms(x, nmix_ref[...]).astype(BF16)

    def proj(c0, c1):
        return _dot(xn, w_in_ref[:, c0 * d:c1 * d])

    def gate(i):
        return _sigmoid(proj(6 + i, 7 + i) + bgate_ref[i:i + 1, :])

    def glu():
        ub_buf[HALO_B:HALO_B + tm, :] = proj(3, 4) * _sigmoid(proj(4, 5))

    def conv_block(c, lb):
        r0 = c * CONV_ROWS
        win_rows = CONV_ROWS + HALO_B
        lanes = slice(lb * LANES, (lb + 1) * LANES)
        win = ub_buf[r0:r0 + win_rows, lanes]
        acc = jnp.zeros((CONV_ROWS, LANES), F32)
        for r in range(SUBLANES):
            rolled = win if r == 0 else pltpu.roll(win, win_rows - r, 0)
            for a in range(win_rows // SUBLANES):
                j = SUBLANES * a + r - (HALO_B - (K_B - 1))
                if 0 <= j < K_B:
                    rows = rolled[SUBLANES * a:SUBLANES * a + CONV_ROWS, :]
                    acc = acc + cbw_ref[j:j + 1, lanes] * rows
        convb_buf[r0:r0 + CONV_ROWS, lanes] = acc + cbb_ref[:, lanes]

    def norm_chunk(c):
        r0 = c * CONV_ROWS
        z = _layernorm_swish(convb_buf[r0:r0 + CONV_ROWS, :], lng_ref[...], lnb_ref[...])
        preb_buf[r0:r0 + CONV_ROWS, :] = z.astype(BF16)

    out = {}

    def branch_a():
        cx = proj(1, 3)
        ua_buf[HALO_A:HALO_A + tm, :] = cx[:, :d] * cx[:, d:]
        conv_a = caw_ref[0:1, :] * ua_buf[HALO_A - 2:HALO_A - 2 + tm, :]
        conv_a = conv_a + caw_ref[1:2, :] * ua_buf[HALO_A - 1:HALO_A - 1 + tm, :]
        conv_a = conv_a + caw_ref[2:3, :] * ua_buf[HALO_A:HALO_A + tm, :]
        out["pre_a"] = (proj(0, 1) * conv_a).astype(BF16)
        na_ref[0] = ua_buf[HALO_A + tm - (K_A - 1):HALO_A + tm, :]
        ua_buf[0:HALO_A, :] = ua_buf[tm:tm + HALO_A, :]

    def branch_a_out():
        out["ya"] = gate(0) * _dot(out["pre_a"], woa_ref[...])

    def query():
        out["q"] = (proj(5, 6) * (dh ** -0.5)).astype(BF16)
        out["heads"] = []

    def head(hd):
        sl = slice(hd * dh, (hd + 1) * dh)
        s = _dot(out["q"][:, sl], kt_ref[0, sl, :])
        e = jnp.exp(s - jnp.max(s, axis=-1, keepdims=True))
        p = e * (1.0 / jnp.sum(e, axis=-1, keepdims=True))
        out["heads"].append(_dot(p.astype(BF16), vb_ref[0, :, sl]))

    def branch_x_out():
        o = jnp.concatenate(out["heads"], axis=-1).astype(BF16)
        out["yx"] = gate(2) * _dot(o, wox_ref[...])

    def gate_b():
        out["gb"] = gate(1)

    mxu_work = [branch_a, branch_a_out, query,
                functools.partial(head, 0), functools.partial(head, 1),
                functools.partial(head, 2), functools.partial(head, 3),
                branch_x_out, gate_b]
    n_chunks = tm // CONV_ROWS
    vector_work = [glu]
    for c in range(n_chunks):
        for lb in range(d // LANES):
            vector_work.append(functools.partial(conv_block, c, lb))
        vector_work.append(functools.partial(norm_chunk, c))
    done = 0
    for i, work in enumerate(vector_work):
        work()
        while done < (i + 1) * len(mxu_work) // len(vector_work):
            mxu_work[done]()
            done += 1

    nb_ref[0] = ub_buf[HALO_B + tm - (K_B - 1):HALO_B + tm, :]
    ub_buf[0:HALO_B, :] = ub_buf[tm:tm + HALO_B, :]

    yb = _dot(preb_buf[...], wob_ref[...])
    merged = out["ya"] + out["gb"] * yb + out["yx"]
    h_ref[0] = x + _dot(merged.astype(BF16), wo_ref[...])

    for src, dst in zip(cast_in, cast_out):
        dst[...] = src[...].astype(BF16)
    _sample_state_update(*state_in, caw_ref, cbw_ref, cbb_ref, lng_ref, lnb_ref,
                         *state_out, state_tokens, d)


def _mixer_call(x, kt, vb, w_in, woa, wob, wox, wo, vecs, cast_weights, sample_state):
    b, t, d = x.shape
    tm = PROMPT_TILE
    nt = t // tm
    steps = b * nt
    n_mem = vb.shape[1]
    d_in = w_in.shape[1]
    step = lambda i, j: i * nt + j
    bf16_rows = 2 * SUBLANES
    cast_specs = []
    for w in cast_weights:
        rows, cols = w.shape
        chunks = steps
        while rows % (chunks * bf16_rows):
            chunks //= 2
        cast_specs.append(pl.BlockSpec(
            (rows // chunks, cols),
            lambda i, j, chunks=chunks: (jnp.minimum(step(i, j), chunks - 1), 0)))

    proj_s, sa, sb = sample_state
    m = proj_s.shape[0]
    tb = SAMPLE_TOKENS
    assert m % tb == 0 and m // tb <= steps
    tok = lambda i, j: jnp.minimum(step(i, j), m // tb - 1)
    row_spec = pl.BlockSpec((tb, d), lambda i, j: (tok(i, j), 0))
    state_a_spec = pl.BlockSpec((tb, K_A - 1, d), lambda i, j: (tok(i, j), 0, 0))
    state_b_spec = pl.BlockSpec((K_B - 1, tb, d), lambda i, j: (0, tok(i, j), 0))
    state_in_specs = [pl.BlockSpec((tb, 5 * d), lambda i, j: (tok(i, j), 0)),
                      state_a_spec, state_b_spec]
    state_out_specs = [row_spec, row_spec, state_a_spec, state_b_spec]
    state_out_shape = [jax.ShapeDtypeStruct((m, d), F32), jax.ShapeDtypeStruct((m, d), F32),
                       jax.ShapeDtypeStruct(sa.shape, F32), jax.ShapeDtypeStruct(sb.shape, F32)]

    kernel = functools.partial(_mixer_kernel, tm=tm, d=d, n_cast=len(cast_weights),
                               state_tokens=tb)
    outs = pl.pallas_call(
        kernel,
        grid=(b, nt),
        in_specs=[
            pl.BlockSpec((1, tm, d), lambda i, j: (i, j, 0)),
            pl.BlockSpec((1, d, n_mem), lambda i, j: (i, 0, 0)),
            pl.BlockSpec((1, n_mem, d), lambda i, j: (i, 0, 0)),
            _resident((d, d_in)),
            _resident((d, d)), _resident((d, d)), _resident((d, d)), _resident((d, d)),
            _resident(vecs.shape),
        ] + cast_specs + state_in_specs,
        out_specs=[
            pl.BlockSpec((1, tm, d), lambda i, j: (i, j, 0)),
            pl.BlockSpec((1, K_A - 1, d), lambda i, j: (i, 0, 0)),
            pl.BlockSpec((1, K_B - 1, d), lambda i, j: (i, 0, 0)),
        ] + cast_specs + state_out_specs,
        out_shape=[
            jax.ShapeDtypeStruct((b, t, d), F32),
            jax.ShapeDtypeStruct((b, K_A - 1, d), F32),
            jax.ShapeDtypeStruct((b, K_B - 1, d), F32),
        ] + [jax.ShapeDtypeStruct(w.shape, BF16) for w in cast_weights] + state_out_shape,
        scratch_shapes=[
            pltpu.VMEM((HALO_A + tm, d), F32),
            pltpu.VMEM((HALO_B + tm, d), F32),
            pltpu.VMEM((tm, d), F32),
            pltpu.VMEM((tm, d), BF16),
        ],
        compiler_params=pltpu.CompilerParams(
            dimension_semantics=("arbitrary", "arbitrary"), vmem_limit_bytes=62 * MIB),
        name="mixer",
    )(x, kt, vb, w_in, woa, wob, wox, wo, vecs, *cast_weights, proj_s, sa, sb)
    n_cast = len(cast_weights)
    return outs[:3], outs[3:3 + n_cast], outs[3 + n_cast:]


def _token_attention(q, k, v):
    n_mem, cs, lanes = k.shape
    scale = (cs // N_HEADS * lanes) ** -0.5
    s = jnp.sum(k * (q * scale)[None], axis=-1, keepdims=True)
    s = s + pltpu.roll(s, N_HEADS, 1)
    e = jnp.exp(s - jnp.max(s, axis=0, keepdims=True))
    den = jnp.sum(e, axis=0)
    return jnp.sum(e * v, axis=0) * (1.0 / den)


def _ffn_kernel(h_ref, vec_ref, wg_ref, wu_ref, wd_ref, *rest, attn_tokens):
    nffn_ref, nfin_ref = _vec(vec_ref, "nffn"), _vec(vec_ref, "nfin")
    if attn_tokens:
        qv_ref, k_ref, v_ref, y_ref, o_ref = rest
        for i in range(attn_tokens):
            o_ref[i] = _token_attention(qv_ref[i], k_ref[i], v_ref[i])
    else:
        (y_ref,) = rest
    h = h_ref[...]
    hn = _rms(h, nffn_ref[...]).astype(BF16)
    gate = _dot(hn, wg_ref[...])
    up = _dot(hn, wu_ref[...])
    act = (gate * _sigmoid(gate) * up).astype(BF16)
    h2 = h + _dot(act, wd_ref[...])
    y_ref[...] = _rms(h2, nfin_ref[...])


def _ffn_call(h, vecs, wg, wu, wd, tm, attn=None):
    m, d = h.shape
    dff = wg.shape[1]
    steps = m // tm
    in_specs = [
        pl.BlockSpec((tm, d), lambda i: (i, 0)),
        _resident(vecs.shape),
        _resident((d, dff)), _resident((d, dff)), _resident((dff, d)),
    ]
    out_specs = [pl.BlockSpec((tm, d), lambda i: (i, 0))]
    out_shape = [jax.ShapeDtypeStruct((m, d), F32)]
    args = [h, vecs, wg, wu, wd]
    attn_tokens = 0
    if attn is not None:
        qv, k, v = attn
        n_tok, n_mem, cs, lanes = k.shape
        assert n_tok % steps == 0
        attn_tokens = n_tok // steps
        chunk_spec = pl.BlockSpec((attn_tokens, cs, lanes), lambda i: (i, 0, 0))
        cache_spec = pl.BlockSpec((attn_tokens, n_mem, cs, lanes), lambda i: (i, 0, 0, 0))
        in_specs += [chunk_spec, cache_spec, cache_spec]
        out_specs.append(chunk_spec)
        out_shape.append(jax.ShapeDtypeStruct((n_tok, cs, lanes), F32))
        args += [qv, k, v]
    return pl.pallas_call(
        functools.partial(_ffn_kernel, attn_tokens=attn_tokens),
        grid=(steps,),
        in_specs=in_specs,
        out_specs=out_specs,
        out_shape=out_shape,
        compiler_params=pltpu.CompilerParams(
            dimension_semantics=("arbitrary",), vmem_limit_bytes=56 * MIB),
        name="ffn_attn" if attn_tokens else "ffn",
    )(*args)


def _sample_state_update(pr_ref, sa_ref, sb_ref, caw_ref, cbw_ref, cbb_ref,
                         lng_ref, lnb_ref, prea_ref, preb_ref, na_ref, nb_ref, tb, d):
    a_b = pr_ref[:, 0:d]
    ua = pr_ref[:, d:2 * d] * pr_ref[:, 2 * d:3 * d]
    ub = pr_ref[:, 3 * d:4 * d] * _sigmoid(pr_ref[:, 4 * d:5 * d])

    conv_b = cbw_ref[K_B - 1:K_B, :] * ub + cbb_ref[...]
    for j in range(K_B - 1):
        conv_b = conv_b + cbw_ref[j:j + 1, :] * sb_ref[j]
    preb_ref[...] = _layernorm_swish(conv_b, lng_ref[...], lnb_ref[...])
    for j in range(K_B - 2):
        nb_ref[j] = sb_ref[j + 1]
    nb_ref[K_B - 2] = ub

    for i in range(tb):
        row = slice(i, i + 1)
        conv_a = (caw_ref[0:1, :] * sa_ref[i, 0:1, :] + caw_ref[1:2, :] * sa_ref[i, 1:2, :]
                  + caw_ref[2:3, :] * ua[row])
        prea_ref[row, :] = a_b[row] * conv_a
        na_ref[i, 0:1, :] = sa_ref[i, 1:2, :]
        na_ref[i, 1:2, :] = ua[row]


def _to_chunks(a, lanes):
    *lead, nh, dh = a.shape
    a = a.reshape(*lead, nh, dh // lanes, lanes)
    a = jnp.swapaxes(a, -3, -2)
    return a.reshape(*lead, (dh // lanes) * nh, lanes)


def _from_chunks(a, nh):
    *lead, cs, lanes = a.shape
    a = a.reshape(*lead, cs // nh, nh, lanes)
    a = jnp.swapaxes(a, -3, -2)
    return a.reshape(*lead, nh, (cs // nh) * lanes)


def _from_tiles(a):
    rows, cs, lanes = a.shape
    flat = a.reshape(rows, cs * lanes)
    blocks = [flat[:, (half * N_HEADS + hd) * lanes:(half * N_HEADS + hd + 1) * lanes]
              for hd in range(N_HEADS) for half in range(cs // N_HEADS)]
    return jnp.concatenate(blocks, axis=-1)


def _spost_kernel(x_ref, prea_ref, preb_ref, o_ref, gl_ref, vec_ref,
                  woa_ref, wob_ref, wox_ref, wo_ref, h_ref, *, d):
    bgate_ref = _vec(vec_ref, "bgate")

    def gate(i):
        return _sigmoid(gl_ref[:, i * d:(i + 1) * d] + bgate_ref[i:i + 1, :])

    ya = _dot(prea_ref[...].astype(BF16), woa_ref[...])
    yb = _dot(preb_ref[...].astype(BF16), wob_ref[...])
    yx = _dot(_from_tiles(o_ref[...]).astype(BF16), wox_ref[...])
    merged = gate(0) * ya + gate(1) * yb + gate(2) * yx
    h_ref[...] = x_ref[...] + _dot(merged.astype(BF16), wo_ref[...])


def _spost_call(x, prea, preb, o_tiles, proj, vecs, woa, wob, wox, wo):
    m, d = x.shape
    kernel = functools.partial(_spost_kernel, d=d)
    full = pl.BlockSpec((m, d), lambda i: (0, 0))
    return pl.pallas_call(
        kernel,
        grid=(1,),
        in_specs=[
            full, full, full,
            pl.BlockSpec(o_tiles.shape, lambda i: (0, 0, 0)),
            pl.BlockSpec((m, 3 * d), lambda i: (0, 2)),
            _resident(vecs.shape),
            _resident((d, d)), _resident((d, d)), _resident((d, d)), _resident((d, d)),
        ],
        out_specs=full,
        out_shape=jax.ShapeDtypeStruct((m, d), F32),
        compiler_params=pltpu.CompilerParams(
            dimension_semantics=("arbitrary",), vmem_limit_bytes=32 * MIB),
        name="sample_post",
    )(x, prea, preb, o_tiles, proj, vecs, woa, wob, wox, wo)


def kernel(x_prompt, x_sample, mem_prompt, cache_mem_k, cache_mem_v, state_conv_a, state_conv_b, norm_mix, w_in, b_gate, conv_a_w, w_out_a, conv_b_w, conv_b_bias, ln_b_g, ln_b_b, w_out_b, norm_mem, w_k, w_v, w_out_x, w_o, norm_ffn, w_ff_gate, w_ff_up, w_ff_down, norm_final):
    depth = w_in.shape[0]
    assert depth == 1, "single-layer step only"
    b, t, d = x_prompt.shape
    sb_, st_, _ = x_sample.shape
    assert st_ == 1
    n_mem = mem_prompt.shape[1]
    nh, dh = w_k.shape[2], w_k.shape[3]
    assert nh == N_HEADS and nh * dh == d
    assert dh == 2 * LANES and 2 * nh == SUBLANES

    wk = w_k[0].reshape(d, d).astype(BF16)
    wv = w_v[0].reshape(d, d).astype(BF16)
    vecs = _pack_vectors(
        d, nmix=norm_mix[0], nmem=norm_mem[0], nffn=norm_ffn[0], nfin=norm_final,
        cbb=conv_b_bias[0], lng=ln_b_g[0], lnb=ln_b_b[0], bgate=b_gate[0],
        caw=conv_a_w[0], cbw=conv_b_w[0])

    xs = x_sample.reshape(sb_, d)
    (k_p, v_p, kt, vb), (proj_s, w_in_b, q_s), (woa, wob, wox, wo) = _prep_call(
        mem_prompt, vecs, wk, wv, xs, w_in[0],
        [w_out_a[0], w_out_b[0], w_out_x[0], w_o[0]], q_block=5)

    (h_p, ca_p, cb_p), (wg, wu, wd), (prea, preb, ca_s, cb_s) = _mixer_call(
        x_prompt, kt, vb, w_in_b, woa, wob, wox, wo, vecs,
        [w_ff_gate[0], w_ff_up[0], w_ff_down[0]],
        (proj_s, state_conv_a[0], jnp.swapaxes(state_conv_b[0], 0, 1)))

    y_p, o_s = _ffn_call(
        h_p.reshape(b * t, d), vecs, wg, wu, wd, PROMPT_TILE,
        attn=(q_s, _to_chunks(cache_mem_k[0], LANES), _to_chunks(cache_mem_v[0], LANES)))

    h_s = _spost_call(xs, prea, preb, o_s, proj_s, vecs, woa, wob, wox, wo)
    (y_s,) = _ffn_call(h_s, vecs, wg, wu, wd, sb_)

    return (y_p.reshape(b, t, d), y_s.reshape(sb_, 1, d),
            _from_chunks(k_p, nh)[None], _from_chunks(v_p, nh)[None],
            ca_p[None], cb_p[None], ca_s[None], jnp.swapaxes(cb_s, 0, 1)[None])
```

```python
import functools

import jax
import jax.numpy as jnp
from jax import lax
from jax.experimental import pallas as pl
from jax.experimental.pallas import tpu as pltpu

F32 = jnp.float32
BF16 = jnp.bfloat16
EPS = 1e-6
LOG2_E = 1.4426950408889634

K_A = 3
K_B = 31
N_HEADS = 4
SUBLANES = 8
LANES = 128
HALO_A = 8
HALO_B = 32
CONV_ROWS = 64

PROMPT_TILE = 512
SAMPLE_TOKENS = 8
MIB = 1024 * 1024


def _dot(a, b):
    return jnp.dot(a, b, preferred_element_type=F32)


def _rms(x, g):
    ms = jnp.mean(x * x, axis=-1, keepdims=True)
    return x * lax.rsqrt(ms + EPS) * g


def _sigmoid(x):
    return 1.0 / (1.0 + jnp.exp2(x * (-LOG2_E)))


def _layernorm_swish(y, g, b):
    mu = jnp.mean(y, axis=-1, keepdims=True)
    yc = y - mu
    var = jnp.mean(yc * yc, axis=-1, keepdims=True)
    z = yc * lax.rsqrt(var + EPS) * g + b
    return z * _sigmoid(z)


VEC_ROWS = {"nmix": (0, 1), "nmem": (1, 1), "nffn": (2, 1), "nfin": (3, 1), "cbb": (4, 1),
            "lng": (5, 1), "lnb": (6, 1), "bgate": (8, 3), "caw": (11, K_A), "cbw": (16, K_B)}
VEC_TABLE_ROWS = 48


def _pack_vectors(d, **named):
    assert set(named) == set(VEC_ROWS)
    pieces, pos = [], 0
    for name, (first, rows) in sorted(VEC_ROWS.items(), key=lambda kv: kv[1]):
        pieces += [jnp.zeros((first - pos, d), F32), named[name].reshape(rows, d)]
        pos = first + rows
    pieces.append(jnp.zeros((VEC_TABLE_ROWS - pos, d), F32))
    return jnp.concatenate([p for p in pieces if p.shape[0]], axis=0)


class _vec:
    def __init__(self, vec_ref, name):
        self.ref = vec_ref
        self.first, self.rows = VEC_ROWS[name]

    def __getitem__(self, idx):
        rows, cols = (slice(None), slice(None)) if idx is Ellipsis else idx
        start = rows.start or 0
        stop = self.rows if rows.stop is None else rows.stop
        return self.ref[self.first + start:self.first + stop, cols]


def _resident(shape):
    nd = len(shape)
    return pl.BlockSpec(shape, lambda *_: (0,) * nd, pipeline_mode=pl.Buffered(1))


def _to_tiles(a, lanes):
    rows, dx = a.shape
    dh = dx // N_HEADS
    blocks = [a[:, hd * dh + half * lanes:hd * dh + (half + 1) * lanes]
              for half in range(dh // lanes) for hd in range(N_HEADS)]
    return jnp.concatenate(blocks, axis=-1).reshape(rows, dx // lanes, lanes)


def _prep_kernel(mem_ref, vec_ref, wk_ref, wv_ref, xs_ref, win_ref, *rest,
                 n_cast, n_batch, q_block):
    cast_in = rest[:n_cast]
    k_ref, v_ref, kt_ref, vb_ref, proj_ref, winb_ref, q_ref, stack_ref = rest[n_cast:]
    g_ref, nmix_ref = _vec(vec_ref, "nmem"), _vec(vec_ref, "nmix")
    j = pl.program_id(0)
    lanes = q_ref.shape[-1]

    xn = _rms(xs_ref[...], nmix_ref[...]).astype(BF16)
    wb = win_ref[...].astype(BF16)
    winb_ref[...] = wb
    proj = _dot(xn, wb)
    proj_ref[...] = proj

    @pl.when(j == q_block)
    def _():
        q_ref[...] = _to_tiles(proj, lanes)

    @pl.when(j < n_batch)
    def _():
        for i, src in enumerate(cast_in):
            stack_ref[i] = src[...].astype(BF16)
        mn = _rms(mem_ref[0], g_ref[...]).astype(BF16)
        k = _dot(mn, wk_ref[...])
        v = _dot(mn, wv_ref[...])
        vb_ref[0] = v.astype(BF16)
        kt_ref[0] = k.T.astype(BF16)
        k_ref[0] = _to_tiles(k, lanes)
        v_ref[0] = _to_tiles(v, lanes)


def _prep_call(mem, vecs, wk, wv, xs, w_in, cast_weights, q_block):
    b, n_mem, d = mem.shape
    m = xs.shape[0]
    dx = wk.shape[1]
    d_in = w_in.shape[1]
    steps = d_in // d
    assert b <= steps
    cs = dx // LANES
    batch = lambda j: jnp.minimum(j, b - 1)
    chunk_spec = pl.BlockSpec((1, n_mem, cs, LANES), lambda j: (batch(j), 0, 0, 0))
    bf16_rows = 2 * SUBLANES
    rows, cols = cast_weights[0].shape
    assert all(w.shape == (rows, cols) for w in cast_weights) and rows % (b * bf16_rows) == 0
    n_cast = len(cast_weights)
    cast_specs = [pl.BlockSpec((rows // b, cols), lambda j: (batch(j), 0))] * n_cast
    outs = pl.pallas_call(
        functools.partial(_prep_kernel, n_cast=len(cast_weights), n_batch=b, q_block=q_block),
        grid=(steps,),
        in_specs=[
            pl.BlockSpec((1, n_mem, d), lambda j: (batch(j), 0, 0)),
            _resident(vecs.shape),
            _resident((d, dx)),
            _resident((d, dx)),
            _resident((m, d)),
            pl.BlockSpec((d, d), lambda j: (0, j)),
        ] + cast_specs,
        out_specs=[
            chunk_spec, chunk_spec,
            pl.BlockSpec((1, dx, n_mem), lambda j: (batch(j), 0, 0)),
            pl.BlockSpec((1, n_mem, dx), lambda j: (batch(j), 0, 0)),
            pl.BlockSpec((m, d), lambda j: (0, j)),
            pl.BlockSpec((d, d), lambda j: (0, j)),
            pl.BlockSpec((m, cs, LANES), lambda j: (0, 0, 0)),
            pl.BlockSpec((n_cast, rows // b, cols), lambda j: (0, batch(j), 0)),
        ],
        out_shape=[
            jax.ShapeDtypeStruct((b, n_mem, cs, LANES), F32),
            jax.ShapeDtypeStruct((b, n_mem, cs, LANES), F32),
            jax.ShapeDtypeStruct((b, dx, n_mem), BF16),
            jax.ShapeDtypeStruct((b, n_mem, dx), BF16),
            jax.ShapeDtypeStruct((m, d_in), F32),
            jax.ShapeDtypeStruct((d, d_in), BF16),
            jax.ShapeDtypeStruct((m, cs, LANES), F32),
            jax.ShapeDtypeStruct((n_cast, rows, cols), BF16),
        ],
        compiler_params=pltpu.CompilerParams(
            dimension_semantics=("arbitrary",), vmem_limit_bytes=48 * MIB),
        name="prep",
    )(mem, vecs, wk, wv, xs, w_in, *cast_weights)
    return outs[:4], outs[4:7], outs[7]


def _mixer_kernel(x_ref, kt_ref, vb_ref, w_in_ref, wout_ref,
                  vec_ref, *rest, tm, d, n_cast, state_tokens):
    nmix_ref, bgate_ref, caw_ref, cbw_ref, cbb_ref, lng_ref, lnb_ref = (
        _vec(vec_ref, name) for name in ("nmix", "bgate", "caw", "cbw", "cbb", "lng", "lnb"))
    rest = list(rest)
    cast_in = [rest.pop(0) for _ in range(n_cast)]
    state_in = [rest.pop(0) for _ in range(3)]
    h_ref, na_ref, nb_ref = [rest.pop(0) for _ in range(3)]
    cast_out = [rest.pop(0) for _ in range(n_cast)]
    state_out = [rest.pop(0) for _ in range(4)]
    ua_buf, ub_buf, convb_buf, preb_buf = rest
    t = pl.program_id(1)
    dh = d // N_HEADS

    @pl.when(t == 0)
    def _():
        ua_buf[0:HALO_A, :] = jnp.zeros((HALO_A, d), F32)
        ub_buf[0:HALO_B, :] = jnp.zeros((HALO_B, d), F32)

    x = x_ref[0]
    xn = _rms(x, nmix_ref[...]).astype(BF16)

    def proj(c0, c1):
        return _dot(xn, w_in_ref[:, c0 * d:c1 * d])

    def gate(i):
        return _sigmoid(proj(6 + i, 7 + i) + bgate_ref[i:i + 1, :])

    def glu():
        ub_buf[HALO_B:HALO_B + tm, :] = proj(3, 4) * _sigmoid(proj(4, 5))

    def conv_block(c, lb):
        r0 = c * CONV_ROWS
        win_rows = CONV_ROWS + HALO_B
        lanes = slice(lb * LANES, (lb + 1) * LANES)
        win = ub_buf[r0:r0 + win_rows, lanes]
        acc = jnp.zeros((CONV_ROWS, LANES), F32)
        for r in range(SUBLANES):
            rolled = win if r == 0 else pltpu.roll(win, win_rows - r, 0)
            for a in range(win_rows // SUBLANES):
                j = SUBLANES * a + r - (HALO_B - (K_B - 1))
                if 0 <= j < K_B:
                    rows = rolled[SUBLANES * a:SUBLANES * a + CONV_ROWS, :]
                    acc = acc + cbw_ref[j:j + 1, lanes] * rows
        convb_buf[r0:r0 + CONV_ROWS, lanes] = acc + cbb_ref[:, lanes]

    def norm_chunk(c):
        r0 = c * CONV_ROWS
        z = _layernorm_swish(convb_buf[r0:r0 + CONV_ROWS, :], lng_ref[...], lnb_ref[...])
        preb_buf[r0:r0 + CONV_ROWS, :] = z.astype(BF16)

    out = {}

    def branch_a():
        cx = proj(1, 3)
        ua_buf[HALO_A:HALO_A + tm, :] = cx[:, :d] * cx[:, d:]
        conv_a = caw_ref[0:1, :] * ua_buf[HALO_A - 2:HALO_A - 2 + tm, :]
        conv_a = conv_a + caw_ref[1:2, :] * ua_buf[HALO_A - 1:HALO_A - 1 + tm, :]
        conv_a = conv_a + caw_ref[2:3, :] * ua_buf[HALO_A:HALO_A + tm, :]
        out["pre_a"] = (proj(0, 1) * conv_a).astype(BF16)
        na_ref[0] = ua_buf[HALO_A + tm - (K_A - 1):HALO_A + tm, :]
        ua_buf[0:HALO_A, :] = ua_buf[tm:tm + HALO_A, :]

    def branch_a_out():
        out["ya"] = gate(0) * _dot(out["pre_a"], wout_ref[0])

    def query():
        out["q"] = (proj(5, 6) * (dh ** -0.5)).astype(BF16)
        out["heads"] = []

    def head(hd):
        sl = slice(hd * dh, (hd + 1) * dh)
        s = _dot(out["q"][:, sl], kt_ref[0, sl, :])
        e = jnp.exp(s - jnp.max(s, axis=-1, keepdims=True))
        p = e * (1.0 / jnp.sum(e, axis=-1, keepdims=True))
        out["heads"].append(_dot(p.astype(BF16), vb_ref[0, :, sl]))

    def branch_x_out():
        o = jnp.concatenate(out["heads"], axis=-1).astype(BF16)
        out["yx"] = gate(2) * _dot(o, wout_ref[2])

    def gate_b():
        out["gb"] = gate(1)

    mxu_work = [branch_a, branch_a_out, query,
                functools.partial(head, 0), functools.partial(head, 1),
                functools.partial(head, 2), functools.partial(head, 3),
                branch_x_out, gate_b]
    n_chunks = tm // CONV_ROWS
    vector_work = [glu]
    for c in range(n_chunks):
        for lb in range(d // LANES):
            vector_work.append(functools.partial(conv_block, c, lb))
        vector_work.append(functools.partial(norm_chunk, c))
    done = 0
    for i, work in enumerate(vector_work):
        work()
        while done < (i + 1) * len(mxu_work) // len(vector_work):
            mxu_work[done]()
            done += 1

    nb_ref[0] = ub_buf[HALO_B + tm - (K_B - 1):HALO_B + tm, :]
    ub_buf[0:HALO_B, :] = ub_buf[tm:tm + HALO_B, :]

    yb = _dot(preb_buf[...], wout_ref[1])
    merged = out["ya"] + out["gb"] * yb + out["yx"]
    h_ref[0] = x + _dot(merged.astype(BF16), wout_ref[3])

    for src, dst in zip(cast_in, cast_out):
        dst[...] = src[...].astype(BF16)
    _sample_state_update(*state_in, caw_ref, cbw_ref, cbb_ref, lng_ref, lnb_ref,
                         *state_out, state_tokens, d)


def _mixer_call(x, kt, vb, w_in, wout, vecs, cast_weights, sample_state):
    b, t, d = x.shape
    tm = PROMPT_TILE
    nt = t // tm
    steps = b * nt
    n_mem = vb.shape[1]
    d_in = w_in.shape[1]
    step = lambda i, j: i * nt + j
    bf16_rows = 2 * SUBLANES
    cast_specs = []
    for w in cast_weights:
        rows, cols = w.shape
        chunks = steps
        while rows % (chunks * bf16_rows):
            chunks //= 2
        cast_specs.append(pl.BlockSpec(
            (rows // chunks, cols),
            lambda i, j, chunks=chunks: (jnp.minimum(step(i, j), chunks - 1), 0)))

    proj_s, sa, sb = sample_state
    m = proj_s.shape[0]
    tb = SAMPLE_TOKENS
    assert m % tb == 0 and m // tb <= steps
    tok = lambda i, j: jnp.minimum(step(i, j), m // tb - 1)
    row_spec = pl.BlockSpec((tb, d), lambda i, j: (tok(i, j), 0))
    state_a_spec = pl.BlockSpec((tb, K_A - 1, d), lambda i, j: (tok(i, j), 0, 0))
    state_b_spec = pl.BlockSpec((K_B - 1, tb, d), lambda i, j: (0, tok(i, j), 0))
    state_in_specs = [pl.BlockSpec((tb, 5 * d), lambda i, j: (tok(i, j), 0)),
                      state_a_spec, state_b_spec]
    state_out_specs = [row_spec, row_spec, state_a_spec, state_b_spec]
    state_out_shape = [jax.ShapeDtypeStruct((m, d), F32), jax.ShapeDtypeStruct((m, d), F32),
                       jax.ShapeDtypeStruct(sa.shape, F32), jax.ShapeDtypeStruct(sb.shape, F32)]

    kernel = functools.partial(_mixer_kernel, tm=tm, d=d, n_cast=len(cast_weights),
                               state_tokens=tb)
    outs = pl.pallas_call(
        kernel,
        grid=(b, nt),
        in_specs=[
            pl.BlockSpec((1, tm, d), lambda i, j: (i, j, 0)),
            pl.BlockSpec((1, d, n_mem), lambda i, j: (i, 0, 0)),
            pl.BlockSpec((1, n_mem, d), lambda i, j: (i, 0, 0)),
            _resident((d, d_in)),
            _resident(wout.shape),
            _resident(vecs.shape),
        ] + cast_specs + state_in_specs,
        out_specs=[
            pl.BlockSpec((1, tm, d), lambda i, j: (i, j, 0)),
            pl.BlockSpec((1, K_A - 1, d), lambda i, j: (i, 0, 0)),
            pl.BlockSpec((1, K_B - 1, d), lambda i, j: (i, 0, 0)),
        ] + cast_specs + state_out_specs,
        out_shape=[
            jax.ShapeDtypeStruct((b, t, d), F32),
            jax.ShapeDtypeStruct((b, K_A - 1, d), F32),
            jax.ShapeDtypeStruct((b, K_B - 1, d), F32),
        ] + [jax.ShapeDtypeStruct(w.shape, BF16) for w in cast_weights] + state_out_shape,
        scratch_shapes=[
            pltpu.VMEM((HALO_A + tm, d), F32),
            pltpu.VMEM((HALO_B + tm, d), F32),
            pltpu.VMEM((tm, d), F32),
            pltpu.VMEM((tm, d), BF16),
        ],
        compiler_params=pltpu.CompilerParams(
            dimension_semantics=("arbitrary", "arbitrary"), vmem_limit_bytes=62 * MIB),
        name="mixer",
    )(x, kt, vb, w_in, wout, vecs, *cast_weights, proj_s, sa, sb)
    n_cast = len(cast_weights)
    return outs[:3], outs[3:3 + n_cast], outs[3 + n_cast:]


def _token_attention(q, k, v):
    n_mem, cs, lanes = k.shape
    scale = (cs // N_HEADS * lanes) ** -0.5
    s = jnp.sum(k * (q * scale)[None], axis=-1, keepdims=True)
    s = s + pltpu.roll(s, N_HEADS, 1)
    e = jnp.exp(s - jnp.max(s, axis=0, keepdims=True))
    den = jnp.sum(e, axis=0)
    return jnp.sum(e * v, axis=0) * (1.0 / den)


def _ffn_kernel(h_ref, vec_ref, wg_ref, wu_ref, wd_ref, *rest, attn_tokens):
    nffn_ref, nfin_ref = _vec(vec_ref, "nffn"), _vec(vec_ref, "nfin")
    if attn_tokens:
        qv_ref, k_ref, v_ref, y_ref, o_ref = rest
        for i in range(attn_tokens):
            o_ref[i] = _token_attention(qv_ref[i], k_ref[i], v_ref[i])
    else:
        (y_ref,) = rest
    h = h_ref[...]
    hn = _rms(h, nffn_ref[...]).astype(BF16)
    gate = _dot(hn, wg_ref[...])
    up = _dot(hn, wu_ref[...])
    act = (gate * _sigmoid(gate) * up).astype(BF16)
    h2 = h + _dot(act, wd_ref[...])
    y_ref[...] = _rms(h2, nfin_ref[...])


def _ffn_call(h, vecs, wg, wu, wd, tm, attn=None):
    m, d = h.shape
    dff = wg.shape[1]
    steps = m // tm
    in_specs = [
        pl.BlockSpec((tm, d), lambda i: (i, 0)),
        _resident(vecs.shape),
        _resident((d, dff)), _resident((d, dff)), _resident((dff, d)),
    ]
    out_specs = [pl.BlockSpec((tm, d), lambda i: (i, 0))]
    out_shape = [jax.ShapeDtypeStruct((m, d), F32)]
    args = [h, vecs, wg, wu, wd]
    attn_tokens = 0
    if attn is not None:
        qv, k, v = attn
        n_tok, n_mem, cs, lanes = k.shape
        assert n_tok % steps == 0
        attn_tokens = n_tok // steps
        chunk_spec = pl.BlockSpec((attn_tokens, cs, lanes), lambda i: (i, 0, 0))
        cache_spec = pl.BlockSpec((attn_tokens, n_mem, cs, lanes), lambda i: (i, 0, 0, 0))
        in_specs += [chunk_spec, cache_spec, cache_spec]
        out_specs.append(chunk_spec)
        out_shape.append(jax.ShapeDtypeStruct((n_tok, cs, lanes), F32))
        args += [qv, k, v]
    return pl.pallas_call(
        functools.partial(_ffn_kernel, attn_tokens=attn_tokens),
        grid=(steps,),
        in_specs=in_specs,
        out_specs=out_specs,
        out_shape=out_shape,
        compiler_params=pltpu.CompilerParams(
            dimension_semantics=("arbitrary",), vmem_limit_bytes=56 * MIB),
        name="ffn_attn" if attn_tokens else "ffn",
    )(*args)


def _sample_state_update(pr_ref, sa_ref, sb_ref, caw_ref, cbw_ref, cbb_ref,
                         lng_ref, lnb_ref, prea_ref, preb_ref, na_ref, nb_ref, tb, d):
    a_b = pr_ref[:, 0:d]
    ua = pr_ref[:, d:2 * d] * pr_ref[:, 2 * d:3 * d]
    ub = pr_ref[:, 3 * d:4 * d] * _sigmoid(pr_ref[:, 4 * d:5 * d])

    conv_b = cbw_ref[K_B - 1:K_B, :] * ub + cbb_ref[...]
    for j in range(K_B - 1):
        conv_b = conv_b + cbw_ref[j:j + 1, :] * sb_ref[j]
    preb_ref[...] = _layernorm_swish(conv_b, lng_ref[...], lnb_ref[...])
    for j in range(K_B - 2):
        nb_ref[j] = sb_ref[j + 1]
    nb_ref[K_B - 2] = ub

    for i in range(tb):
        row = slice(i, i + 1)
        conv_a = (caw_ref[0:1, :] * sa_ref[i, 0:1, :] + caw_ref[1:2, :] * sa_ref[i, 1:2, :]
                  + caw_ref[2:3, :] * ua[row])
        prea_ref[row, :] = a_b[row] * conv_a
        na_ref[i, 0:1, :] = sa_ref[i, 1:2, :]
        na_ref[i, 1:2, :] = ua[row]


def _to_chunks(a, lanes):
    *lead, nh, dh = a.shape
    a = a.reshape(*lead, nh, dh // lanes, lanes)
    a = jnp.swapaxes(a, -3, -2)
    return a.reshape(*lead, (dh // lanes) * nh, lanes)


def _from_chunks(a, nh):
    *lead, cs, lanes = a.shape
    a = a.reshape(*lead, cs // nh, nh, lanes)
    a = jnp.swapaxes(a, -3, -2)
    return a.reshape(*lead, nh, (cs // nh) * lanes)


def _from_tiles(a):
    rows, cs, lanes = a.shape
    flat = a.reshape(rows, cs * lanes)
    blocks = [flat[:, (half * N_HEADS + hd) * lanes:(half * N_HEADS + hd + 1) * lanes]
              for hd in range(N_HEADS) for half in range(cs // N_HEADS)]
    return jnp.concatenate(blocks, axis=-1)


def _spost_kernel(x_ref, prea_ref, preb_ref, o_ref, gl_ref, vec_ref,
                  wout_ref, h_ref, *, d):
    bgate_ref = _vec(vec_ref, "bgate")

    def gate(i):
        return _sigmoid(gl_ref[:, i * d:(i + 1) * d] + bgate_ref[i:i + 1, :])

    ya = _dot(prea_ref[...].astype(BF16), wout_ref[0])
    yb = _dot(preb_ref[...].astype(BF16), wout_ref[1])
    yx = _dot(_from_tiles(o_ref[...]).astype(BF16), wout_ref[2])
    merged = gate(0) * ya + gate(1) * yb + gate(2) * yx
    h_ref[...] = x_ref[...] + _dot(merged.astype(BF16), wout_ref[3])


def _spost_call(x, prea, preb, o_tiles, proj, vecs, wout):
    m, d = x.shape
    kernel = functools.partial(_spost_kernel, d=d)
    full = pl.BlockSpec((m, d), lambda i: (0, 0))
    return pl.pallas_call(
        kernel,
        grid=(1,),
        in_specs=[
            full, full, full,
            pl.BlockSpec(o_tiles.shape, lambda i: (0, 0, 0)),
            pl.BlockSpec((m, 3 * d), lambda i: (0, 2)),
            _resident(vecs.shape),
            _resident(wout.shape),
        ],
        out_specs=full,
        out_shape=jax.ShapeDtypeStruct((m, d), F32),
        compiler_params=pltpu.CompilerParams(
            dimension_semantics=("arbitrary",), vmem_limit_bytes=32 * MIB),
        name="sample_post",
    )(x, prea, preb, o_tiles, proj, vecs, wout)


def kernel(x_prompt, x_sample, mem_prompt, cache_mem_k, cache_mem_v, state_conv_a, state_conv_b, norm_mix, w_in, b_gate, conv_a_w, w_out_a, conv_b_w, conv_b_bias, ln_b_g, ln_b_b, w_out_b, norm_mem, w_k, w_v, w_out_x, w_o, norm_ffn, w_ff_gate, w_ff_up, w_ff_down, norm_final):
    depth = w_in.shape[0]
    assert depth == 1, "single-layer step only"
    b, t, d = x_prompt.shape
    sb_, st_, _ = x_sample.shape
    assert st_ == 1
    n_mem = mem_prompt.shape[1]
    nh, dh = w_k.shape[2], w_k.shape[3]
    assert nh == N_HEADS and nh * dh == d
    assert dh == 2 * LANES and 2 * nh == SUBLANES

    wk = w_k[0].reshape(d, d).astype(BF16)
    wv = w_v[0].reshape(d, d).astype(BF16)
    vecs = _pack_vectors(
        d, nmix=norm_mix[0], nmem=norm_mem[0], nffn=norm_ffn[0], nfin=norm_final,
        cbb=conv_b_bias[0], lng=ln_b_g[0], lnb=ln_b_b[0], bgate=b_gate[0],
        caw=conv_a_w[0], cbw=conv_b_w[0])

    xs = x_sample.reshape(sb_, d)
    (k_p, v_p, kt, vb), (proj_s, w_in_b, q_s), wout = _prep_call(
        mem_prompt, vecs, wk, wv, xs, w_in[0],
        [w_out_a[0], w_out_b[0], w_out_x[0], w_o[0]], q_block=5)

    (h_p, ca_p, cb_p), (wg, wu, wd), (prea, preb, ca_s, cb_s) = _mixer_call(
        x_prompt, kt, vb, w_in_b, wout, vecs,
        [w_ff_gate[0], w_ff_up[0], w_ff_down[0]],
        (proj_s, state_conv_a[0], jnp.swapaxes(state_conv_b[0], 0, 1)))

    y_p, o_s = _ffn_call(
        h_p.reshape(b * t, d), vecs, wg, wu, wd, PROMPT_TILE,
        attn=(q_s, _to_chunks(cache_mem_k[0], LANES), _to_chunks(cache_mem_v[0], LANES)))

    h_s = _spost_call(xs, prea, preb, o_s, proj_s, vecs, wout)
    (y_s,) = _ffn_call(h_s, vecs, wg, wu, wd, sb_)

    return (y_p.reshape(b, t, d), y_s.reshape(sb_, 1, d),
            _from_chunks(k_p, nh)[None], _from_chunks(v_p, nh)[None],
            ca_p[None], cb_p[None], ca_s[None], jnp.swapaxes(cb_s, 0, 1)[None])
```

```python
import functools

import jax
import jax.numpy as jnp
from jax import lax
from jax.experimental import pallas as pl
from jax.experimental.pallas import tpu as pltpu

F32 = jnp.float32
BF16 = jnp.bfloat16
EPS = 1e-6
LOG2_E = 1.4426950408889634

K_A = 3
K_B = 31
N_HEADS = 4
SUBLANES = 8
LANES = 128
HALO_A = 8
HALO_B = 32
CONV_ROWS = 64

PROMPT_TILE = 512
SAMPLE_TOKENS = 8
MIB = 1024 * 1024


def _dot(a, b):
    return jnp.dot(a, b, preferred_element_type=F32)


def _rms(x, g):
    ms = jnp.mean(x * x, axis=-1, keepdims=True)
    return x * lax.rsqrt(ms + EPS) * g


def _sigmoid(x):
    return 1.0 / (1.0 + jnp.exp2(x * (-LOG2_E)))


def _layernorm_swish(y, g, b):
    mu = jnp.mean(y, axis=-1, keepdims=True)
    yc = y - mu
    var = jnp.mean(yc * yc, axis=-1, keepdims=True)
    z = yc * lax.rsqrt(var + EPS) * g + b
    return z * _sigmoid(z)


VEC_ROWS = {"nmix": (0, 1), "nmem": (1, 1), "nffn": (2, 1), "nfin": (3, 1), "cbb": (4, 1),
            "lng": (5, 1), "lnb": (6, 1), "bgate": (8, 3), "caw": (11, K_A), "cbw": (16, K_B)}
VEC_TABLE_ROWS = 48


def _pack_vectors(d, **named):
    assert set(named) == set(VEC_ROWS)
    pieces, pos = [], 0
    for name, (first, rows) in sorted(VEC_ROWS.items(), key=lambda kv: kv[1]):
        pieces += [jnp.zeros((first - pos, d), F32), named[name].reshape(rows, d)]
        pos = first + rows
    pieces.append(jnp.zeros((VEC_TABLE_ROWS - pos, d), F32))
    return jnp.concatenate([p for p in pieces if p.shape[0]], axis=0)


class _vec:
    def __init__(self, vec_ref, name):
        self.ref = vec_ref
        self.first, self.rows = VEC_ROWS[name]

    def __getitem__(self, idx):
        rows, cols = (slice(None), slice(None)) if idx is Ellipsis else idx
        start = rows.start or 0
        stop = self.rows if rows.stop is None else rows.stop
        return self.ref[self.first + start:self.first + stop, cols]


def _resident(shape):
    nd = len(shape)
    return pl.BlockSpec(shape, lambda *_: (0,) * nd, pipeline_mode=pl.Buffered(1))


def _to_tiles(a, lanes):
    rows, dx = a.shape
    dh = dx // N_HEADS
    blocks = [a[:, hd * dh + half * lanes:hd * dh + (half + 1) * lanes]
              for half in range(dh // lanes) for hd in range(N_HEADS)]
    return jnp.concatenate(blocks, axis=-1).reshape(rows, dx // lanes, lanes)


def _prep_kernel(mem_ref, vec_ref, wk_ref, wv_ref, xs_ref, win_ref, *rest,
                 n_cast, n_batch, q_block):
    cast_in = rest[:n_cast]
    (k_ref, v_ref, kt_ref, vb_ref, proj_ref, winb_ref, q_ref, stack_ref,
     wk_buf, wv_buf) = rest[n_cast:]
    g_ref, nmix_ref = _vec(vec_ref, "nmem"), _vec(vec_ref, "nmix")
    j = pl.program_id(0)
    lanes = q_ref.shape[-1]

    @pl.when(j == 0)
    def _():
        wk_buf[...] = _from_tiles(wk_ref[...]).astype(BF16)
        wv_buf[...] = _from_tiles(wv_ref[...]).astype(BF16)

    xn = _rms(xs_ref[...], nmix_ref[...]).astype(BF16)
    wb = win_ref[...].astype(BF16)
    winb_ref[...] = wb
    proj = _dot(xn, wb)
    proj_ref[...] = proj

    @pl.when(j == q_block)
    def _():
        q_ref[...] = _to_tiles(proj, lanes)

    @pl.when(j < n_batch)
    def _():
        for i, src in enumerate(cast_in):
            stack_ref[i] = src[...].astype(BF16)
        mn = _rms(mem_ref[0], g_ref[...]).astype(BF16)
        k = _dot(mn, wk_buf[...])
        v = _dot(mn, wv_buf[...])
        vb_ref[0] = v.astype(BF16)
        kt_ref[0] = k.T.astype(BF16)
        k_ref[0] = _to_tiles(k, lanes)
        v_ref[0] = _to_tiles(v, lanes)


def _prep_call(mem, vecs, wk, wv, xs, w_in, cast_weights, q_block):
    b, n_mem, d = mem.shape
    m = xs.shape[0]
    dx = wk.shape[1] * wk.shape[2]
    d_in = w_in.shape[1]
    steps = d_in // d
    assert b <= steps
    cs = dx // LANES
    batch = lambda j: jnp.minimum(j, b - 1)
    chunk_spec = pl.BlockSpec((1, n_mem, cs, LANES), lambda j: (batch(j), 0, 0, 0))
    bf16_rows = 2 * SUBLANES
    rows, cols = cast_weights[0].shape
    assert all(w.shape == (rows, cols) for w in cast_weights) and rows % (b * bf16_rows) == 0
    n_cast = len(cast_weights)
    cast_specs = [pl.BlockSpec((rows // b, cols), lambda j: (batch(j), 0))] * n_cast
    outs = pl.pallas_call(
        functools.partial(_prep_kernel, n_cast=len(cast_weights), n_batch=b, q_block=q_block),
        grid=(steps,),
        in_specs=[
            pl.BlockSpec((1, n_mem, d), lambda j: (batch(j), 0, 0)),
            _resident(vecs.shape),
            _resident(wk.shape),
            _resident(wv.shape),
            _resident((m, d)),
            pl.BlockSpec((d, d), lambda j: (0, j)),
        ] + cast_specs,
        out_specs=[
            chunk_spec, chunk_spec,
            pl.BlockSpec((1, dx, n_mem), lambda j: (batch(j), 0, 0)),
            pl.BlockSpec((1, n_mem, dx), lambda j: (batch(j), 0, 0)),
            pl.BlockSpec((m, d), lambda j: (0, j)),
            pl.BlockSpec((d, d), lambda j: (0, j)),
            pl.BlockSpec((m, cs, LANES), lambda j: (0, 0, 0)),
            pl.BlockSpec((n_cast, rows // b, cols), lambda j: (0, batch(j), 0)),
        ],
        out_shape=[
            jax.ShapeDtypeStruct((b, n_mem, cs, LANES), F32),
            jax.ShapeDtypeStruct((b, n_mem, cs, LANES), F32),
            jax.ShapeDtypeStruct((b, dx, n_mem), BF16),
            jax.ShapeDtypeStruct((b, n_mem, dx), BF16),
            jax.ShapeDtypeStruct((m, d_in), F32),
            jax.ShapeDtypeStruct((d, d_in), BF16),
            jax.ShapeDtypeStruct((m, cs, LANES), F32),
            jax.ShapeDtypeStruct((n_cast, rows, cols), BF16),
        ],
        scratch_shapes=[pltpu.VMEM((d, dx), BF16), pltpu.VMEM((d, dx), BF16)],
        compiler_params=pltpu.CompilerParams(
            dimension_semantics=("arbitrary",), vmem_limit_bytes=48 * MIB),
        name="prep",
    )(mem, vecs, wk, wv, xs, w_in, *cast_weights)
    return outs[:4], outs[4:7], outs[7]


def _mixer_kernel(x_ref, kt_ref, vb_ref, w_in_ref, wout_ref,
                  vec_ref, *rest, tm, d, n_cast, state_tokens):
    nmix_ref, bgate_ref, caw_ref, cbw_ref, cbb_ref, lng_ref, lnb_ref = (
        _vec(vec_ref, name) for name in ("nmix", "bgate", "caw", "cbw", "cbb", "lng", "lnb"))
    rest = list(rest)
    cast_in = [rest.pop(0) for _ in range(n_cast)]
    state_in = [rest.pop(0) for _ in range(3)]
    h_ref, na_ref, nb_ref = [rest.pop(0) for _ in range(3)]
    cast_out = [rest.pop(0) for _ in range(n_cast)]
    state_out = [rest.pop(0) for _ in range(4)]
    ua_buf, ub_buf, convb_buf, preb_buf = rest
    t = pl.program_id(1)
    dh = d // N_HEADS

    @pl.when(t == 0)
    def _():
        ua_buf[0:HALO_A, :] = jnp.zeros((HALO_A, d), F32)
        ub_buf[0:HALO_B, :] = jnp.zeros((HALO_B, d), F32)

    x = x_ref[0]
    xn = _rms(x, nmix_ref[...]).astype(BF16)

    def proj(c0, c1):
        return _dot(xn, w_in_ref[:, c0 * d:c1 * d])

    def gate(i):
        return _sigmoid(proj(6 + i, 7 + i) + bgate_ref[i:i + 1, :])

    def glu():
        ub_buf[HALO_B:HALO_B + tm, :] = proj(3, 4) * _sigmoid(proj(4, 5))

    def conv_block(c, lb):
        r0 = c * CONV_ROWS
        win_rows = CONV_ROWS + HALO_B
        lanes = slice(lb * LANES, (lb + 1) * LANES)
        win = ub_buf[r0:r0 + win_rows, lanes]
        acc = jnp.zeros((CONV_ROWS, LANES), F32)
        for r in range(SUBLANES):
            rolled = win if r == 0 else pltpu.roll(win, win_rows - r, 0)
            for a in range(win_rows // SUBLANES):
                j = SUBLANES * a + r - (HALO_B - (K_B - 1))
                if 0 <= j < K_B:
                    rows = rolled[SUBLANES * a:SUBLANES * a + CONV_ROWS, :]
                    acc = acc + cbw_ref[j:j + 1, lanes] * rows
        convb_buf[r0:r0 + CONV_ROWS, lanes] = acc + cbb_ref[:, lanes]

    def norm_chunk(c):
        r0 = c * CONV_ROWS
        z = _layernorm_swish(convb_buf[r0:r0 + CONV_ROWS, :], lng_ref[...], lnb_ref[...])
        preb_buf[r0:r0 + CONV_ROWS, :] = z.astype(BF16)

    out = {}

    def branch_a():
        cx = proj(1, 3)
        ua_buf[HALO_A:HALO_A + tm, :] = cx[:, :d] * cx[:, d:]
        conv_a = caw_ref[0:1, :] * ua_buf[HALO_A - 2:HALO_A - 2 + tm, :]
        conv_a = conv_a + caw_ref[1:2, :] * ua_buf[HALO_A - 1:HALO_A - 1 + tm, :]
        conv_a = conv_a + caw_ref[2:3, :] * ua_buf[HALO_A:HALO_A + tm, :]
        out["pre_a"] = (proj(0, 1) * conv_a).astype(BF16)
        na_ref[0] = ua_buf[HALO_A + tm - (K_A - 1):HALO_A + tm, :]
        ua_buf[0:HALO_A, :] = ua_buf[tm:tm + HALO_A, :]

    def branch_a_out():
        out["ya"] = gate(0) * _dot(out["pre_a"], wout_ref[0])

    def query():
        out["q"] = (proj(5, 6) * (dh ** -0.5)).astype(BF16)
        out["heads"] = []

    def head(hd):
        sl = slice(hd * dh, (hd + 1) * dh)
        s = _dot(out["q"][:, sl], kt_ref[0, sl, :])
        e = jnp.exp(s - jnp.max(s, axis=-1, keepdims=True))
        p = e * (1.0 / jnp.sum(e, axis=-1, keepdims=True))
        out["heads"].append(_dot(p.astype(BF16), vb_ref[0, :, sl]))

    def branch_x_out():
        o = jnp.concatenate(out["heads"], axis=-1).astype(BF16)
        out["yx"] = gate(2) * _dot(o, wout_ref[2])

    def gate_b():
        out["gb"] = gate(1)

    mxu_work = [branch_a, branch_a_out, query,
                functools.partial(head, 0), functools.partial(head, 1),
                functools.partial(head, 2), functools.partial(head, 3),
                branch_x_out, gate_b]
    n_chunks = tm // CONV_ROWS
    vector_work = [glu]
    for c in range(n_chunks):
        for lb in range(d // LANES):
            vector_work.append(functools.partial(conv_block, c, lb))
        vector_work.append(functools.partial(norm_chunk, c))
    done = 0
    for i, work in enumerate(vector_work):
        work()
        while done < (i + 1) * len(mxu_work) // len(vector_work):
            mxu_work[done]()
            done += 1

    nb_ref[0] = ub_buf[HALO_B + tm - (K_B - 1):HALO_B + tm, :]
    ub_buf[0:HALO_B, :] = ub_buf[tm:tm + HALO_B, :]

    yb = _dot(preb_buf[...], wout_ref[1])
    merged = out["ya"] + out["gb"] * yb + out["yx"]
    h_ref[0] = x + _dot(merged.astype(BF16), wout_ref[3])

    for src, dst in zip(cast_in, cast_out):
        dst[...] = src[...].astype(BF16)
    _sample_state_update(*state_in, caw_ref, cbw_ref, cbb_ref, lng_ref, lnb_ref,
                         *state_out, state_tokens, d)


def _mixer_call(x, kt, vb, w_in, wout, vecs, cast_weights, sample_state):
    b, t, d = x.shape
    tm = PROMPT_TILE
    nt = t // tm
    steps = b * nt
    n_mem = vb.shape[1]
    d_in = w_in.shape[1]
    step = lambda i, j: i * nt + j
    bf16_rows = 2 * SUBLANES
    cast_specs = []
    for w in cast_weights:
        rows, cols = w.shape
        chunks = steps
        while rows % (chunks * bf16_rows):
            chunks //= 2
        cast_specs.append(pl.BlockSpec(
            (rows // chunks, cols),
            lambda i, j, chunks=chunks: (jnp.minimum(step(i, j), chunks - 1), 0)))

    proj_s, sa, sb = sample_state
    m = proj_s.shape[0]
    tb = SAMPLE_TOKENS
    assert m % tb == 0 and m // tb <= steps
    tok = lambda i, j: jnp.minimum(step(i, j), m // tb - 1)
    row_spec = pl.BlockSpec((tb, d), lambda i, j: (tok(i, j), 0))
    state_a_spec = pl.BlockSpec((tb, K_A - 1, d), lambda i, j: (tok(i, j), 0, 0))
    state_b_spec = pl.BlockSpec((K_B - 1, tb, d), lambda i, j: (0, tok(i, j), 0))
    state_in_specs = [pl.BlockSpec((tb, 5 * d), lambda i, j: (tok(i, j), 0)),
                      state_a_spec, state_b_spec]
    state_out_specs = [row_spec, row_spec, state_a_spec, state_b_spec]
    state_out_shape = [jax.ShapeDtypeStruct((m, d), F32), jax.ShapeDtypeStruct((m, d), F32),
                       jax.ShapeDtypeStruct(sa.shape, F32), jax.ShapeDtypeStruct(sb.shape, F32)]

    kernel = functools.partial(_mixer_kernel, tm=tm, d=d, n_cast=len(cast_weights),
                               state_tokens=tb)
    outs = pl.pallas_call(
        kernel,
        grid=(b, nt),
        in_specs=[
            pl.BlockSpec((1, tm, d), lambda i, j: (i, j, 0)),
            pl.BlockSpec((1, d, n_mem), lambda i, j: (i, 0, 0)),
            pl.BlockSpec((1, n_mem, d), lambda i, j: (i, 0, 0)),
            _resident((d, d_in)),
            _resident(wout.shape),
            _resident(vecs.shape),
        ] + cast_specs + state_in_specs,
        out_specs=[
            pl.BlockSpec((1, tm, d), lambda i, j: (i, j, 0)),
            pl.BlockSpec((1, K_A - 1, d), lambda i, j: (i, 0, 0)),
            pl.BlockSpec((1, K_B - 1, d), lambda i, j: (i, 0, 0)),
        ] + cast_specs + state_out_specs,
        out_shape=[
            jax.ShapeDtypeStruct((b, t, d), F32),
            jax.ShapeDtypeStruct((b, K_A - 1, d), F32),
            jax.ShapeDtypeStruct((b, K_B - 1, d), F32),
        ] + [jax.ShapeDtypeStruct(w.shape, BF16) for w in cast_weights] + state_out_shape,
        scratch_shapes=[
            pltpu.VMEM((HALO_A + tm, d), F32),
            pltpu.VMEM((HALO_B + tm, d), F32),
            pltpu.VMEM((tm, d), F32),
            pltpu.VMEM((tm, d), BF16),
        ],
        compiler_params=pltpu.CompilerParams(
            dimension_semantics=("arbitrary", "arbitrary"), vmem_limit_bytes=62 * MIB),
        name="mixer",
    )(x, kt, vb, w_in, wout, vecs, *cast_weights, proj_s, sa, sb)
    n_cast = len(cast_weights)
    return outs[:3], outs[3:3 + n_cast], outs[3 + n_cast:]


def _token_attention(q, k, v):
    n_mem, cs, lanes = k.shape
    scale = (cs // N_HEADS * lanes) ** -0.5
    s = jnp.sum(k * (q * scale)[None], axis=-1, keepdims=True)
    s = s + pltpu.roll(s, N_HEADS, 1)
    e = jnp.exp(s - jnp.max(s, axis=0, keepdims=True))
    den = jnp.sum(e, axis=0)
    return jnp.sum(e * v, axis=0) * (1.0 / den)


def _ffn_kernel(h_ref, vec_ref, wg_ref, wu_ref, wd_ref, *rest, attn_tokens):
    nffn_ref, nfin_ref = _vec(vec_ref, "nffn"), _vec(vec_ref, "nfin")
    if attn_tokens:
        qv_ref, k_ref, v_ref, y_ref, o_ref = rest
        for i in range(attn_tokens):
            o_ref[i] = _token_attention(qv_ref[i], k_ref[i], v_ref[i])
    else:
        (y_ref,) = rest
    h = h_ref[...]
    hn = _rms(h, nffn_ref[...]).astype(BF16)
    gate = _dot(hn, wg_ref[...])
    up = _dot(hn, wu_ref[...])
    act = (gate * _sigmoid(gate) * up).astype(BF16)
    h2 = h + _dot(act, wd_ref[...])
    y_ref[...] = _rms(h2, nfin_ref[...])


def _ffn_call(h, vecs, wg, wu, wd, tm, attn=None):
    m, d = h.shape
    dff = wg.shape[1]
    steps = m // tm
    in_specs = [
        pl.BlockSpec((tm, d), lambda i: (i, 0)),
        _resident(vecs.shape),
        _resident((d, dff)), _resident((d, dff)), _resident((dff, d)),
    ]
    out_specs = [pl.BlockSpec((tm, d), lambda i: (i, 0))]
    out_shape = [jax.ShapeDtypeStruct((m, d), F32)]
    args = [h, vecs, wg, wu, wd]
    attn_tokens = 0
    if attn is not None:
        qv, k, v = attn
        n_tok, n_mem, cs, lanes = k.shape
        assert n_tok % steps == 0
        attn_tokens = n_tok // steps
        chunk_spec = pl.BlockSpec((attn_tokens, cs, lanes), lambda i: (i, 0, 0))
        cache_spec = pl.BlockSpec((attn_tokens, n_mem, cs, lanes), lambda i: (i, 0, 0, 0))
        in_specs += [chunk_spec, cache_spec, cache_spec]
        out_specs.append(chunk_spec)
        out_shape.append(jax.ShapeDtypeStruct((n_tok, cs, lanes), F32))
        args += [qv, k, v]
    return pl.pallas_call(
        functools.partial(_ffn_kernel, attn_tokens=attn_tokens),
        grid=(steps,),
        in_specs=in_specs,
        out_specs=out_specs,
        out_shape=out_shape,
        compiler_params=pltpu.CompilerParams(
            dimension_semantics=("arbitrary",), vmem_limit_bytes=56 * MIB),
        name="ffn_attn" if attn_tokens else "ffn",
    )(*args)


def _sample_state_update(pr_ref, sa_ref, sb_ref, caw_ref, cbw_ref, cbb_ref,
                         lng_ref, lnb_ref, prea_ref, preb_ref, na_ref, nb_ref, tb, d):
    a_b = pr_ref[:, 0:d]
    ua = pr_ref[:, d:2 * d] * pr_ref[:, 2 * d:3 * d]
    ub = pr_ref[:, 3 * d:4 * d] * _sigmoid(pr_ref[:, 4 * d:5 * d])

    conv_b = cbw_ref[K_B - 1:K_B, :] * ub + cbb_ref[...]
    for j in range(K_B - 1):
        conv_b = conv_b + cbw_ref[j:j + 1, :] * sb_ref[j]
    preb_ref[...] = _layernorm_swish(conv_b, lng_ref[...], lnb_ref[...])
    for j in range(K_B - 2):
        nb_ref[j] = sb_ref[j + 1]
    nb_ref[K_B - 2] = ub

    for i in range(tb):
        row = slice(i, i + 1)
        conv_a = (caw_ref[0:1, :] * sa_ref[i, 0:1, :] + caw_ref[1:2, :] * sa_ref[i, 1:2, :]
                  + caw_ref[2:3, :] * ua[row])
        prea_ref[row, :] = a_b[row] * conv_a
        na_ref[i, 0:1, :] = sa_ref[i, 1:2, :]
        na_ref[i, 1:2, :] = ua[row]


def _to_chunks(a, lanes):
    *lead, nh, dh = a.shape
    a = a.reshape(*lead, nh, dh // lanes, lanes)
    a = jnp.swapaxes(a, -3, -2)
    return a.reshape(*lead, (dh // lanes) * nh, lanes)


def _from_chunks(a, nh):
    *lead, cs, lanes = a.shape
    a = a.reshape(*lead, cs // nh, nh, lanes)
    a = jnp.swapaxes(a, -3, -2)
    return a.reshape(*lead, nh, (cs // nh) * lanes)


def _from_tiles(a):
    rows, cs, lanes = a.shape
    flat = a.reshape(rows, cs * lanes)
    blocks = [flat[:, (half * N_HEADS + hd) * lanes:(half * N_HEADS + hd + 1) * lanes]
              for hd in range(N_HEADS) for half in range(cs // N_HEADS)]
    return jnp.concatenate(blocks, axis=-1)


def _spost_kernel(x_ref, prea_ref, preb_ref, o_ref, gl_ref, vec_ref,
                  wout_ref, h_ref, *, d):
    bgate_ref = _vec(vec_ref, "bgate")

    def gate(i):
        return _sigmoid(gl_ref[:, i * d:(i + 1) * d] + bgate_ref[i:i + 1, :])

    ya = _dot(prea_ref[...].astype(BF16), wout_ref[0])
    yb = _dot(preb_ref[...].astype(BF16), wout_ref[1])
    yx = _dot(_from_tiles(o_ref[...]).astype(BF16), wout_ref[2])
    merged = gate(0) * ya + gate(1) * yb + gate(2) * yx
    h_ref[...] = x_ref[...] + _dot(merged.astype(BF16), wout_ref[3])


def _spost_call(x, prea, preb, o_tiles, proj, vecs, wout):
    m, d = x.shape
    kernel = functools.partial(_spost_kernel, d=d)
    full = pl.BlockSpec((m, d), lambda i: (0, 0))
    return pl.pallas_call(
        kernel,
        grid=(1,),
        in_specs=[
            full, full, full,
            pl.BlockSpec(o_tiles.shape, lambda i: (0, 0, 0)),
            pl.BlockSpec((m, 3 * d), lambda i: (0, 2)),
            _resident(vecs.shape),
            _resident(wout.shape),
        ],
        out_specs=full,
        out_shape=jax.ShapeDtypeStruct((m, d), F32),
        compiler_params=pltpu.CompilerParams(
            dimension_semantics=("arbitrary",), vmem_limit_bytes=32 * MIB),
        name="sample_post",
    )(x, prea, preb, o_tiles, proj, vecs, wout)


def kernel(x_prompt, x_sample, mem_prompt, cache_mem_k, cache_mem_v, state_conv_a, state_conv_b, norm_mix, w_in, b_gate, conv_a_w, w_out_a, conv_b_w, conv_b_bias, ln_b_g, ln_b_b, w_out_b, norm_mem, w_k, w_v, w_out_x, w_o, norm_ffn, w_ff_gate, w_ff_up, w_ff_down, norm_final):
    depth = w_in.shape[0]
    assert depth == 1, "single-layer step only"
    b, t, d = x_prompt.shape
    sb_, st_, _ = x_sample.shape
    assert st_ == 1
    n_mem = mem_prompt.shape[1]
    nh, dh = w_k.shape[2], w_k.shape[3]
    assert nh == N_HEADS and nh * dh == d
    assert dh == 2 * LANES and 2 * nh == SUBLANES

    wk, wv = _to_chunks(w_k[0], LANES), _to_chunks(w_v[0], LANES)
    vecs = _pack_vectors(
        d, nmix=norm_mix[0], nmem=norm_mem[0], nffn=norm_ffn[0], nfin=norm_final,
        cbb=conv_b_bias[0], lng=ln_b_g[0], lnb=ln_b_b[0], bgate=b_gate[0],
        caw=conv_a_w[0], cbw=conv_b_w[0])

    xs = x_sample.reshape(sb_, d)
    (k_p, v_p, kt, vb), (proj_s, w_in_b, q_s), wout = _prep_call(
        mem_prompt, vecs, wk, wv, xs, w_in[0],
        [w_out_a[0], w_out_b[0], w_out_x[0], w_o[0]], q_block=5)

    (h_p, ca_p, cb_p), (wg, wu, wd), (prea, preb, ca_s, cb_s) = _mixer_call(
        x_prompt, kt, vb, w_in_b, wout, vecs,
        [w_ff_gate[0], w_ff_up[0], w_ff_down[0]],
        (proj_s, state_conv_a[0], jnp.swapaxes(state_conv_b[0], 0, 1)))

    y_p, o_s = _ffn_call(
        h_p.reshape(b * t, d), vecs, wg, wu, wd, PROMPT_TILE,
        attn=(q_s, _to_chunks(cache_mem_k[0], LANES), _to_chunks(cache_mem_v[0], LANES)))

    h_s = _spost_call(xs, prea, preb, o_s, proj_s, vecs, wout)
    (y_s,) = _ffn_call(h_s, vecs, wg, wu, wd, sb_)

    return (y_p.reshape(b, t, d), y_s.reshape(sb_, 1, d),
            _from_chunks(k_p, nh)[None], _from_chunks(v_p, nh)[None],
            ca_p[None], cb_p[None], ca_s[None], jnp.swapaxes(cb_s, 0, 1)[None])
```

```python
import functools

import jax
import jax.numpy as jnp
from jax import lax
from jax.experimental import pallas as pl
from jax.experimental.pallas import tpu as pltpu

F32 = jnp.float32
BF16 = jnp.bfloat16
EPS = 1e-6
LOG2_E = 1.4426950408889634

K_A = 3
K_B = 31
N_HEADS = 4
SUBLANES = 8
LANES = 128
HALO_A = 8
HALO_B = 32
CONV_ROWS = 64

PROMPT_TILE = 512
SAMPLE_TOKENS = 8
MIB = 1024 * 1024


def _dot(a, b):
    return jnp.dot(a, b, preferred_element_type=F32)


def _rms(x, g):
    ms = jnp.mean(x * x, axis=-1, keepdims=True)
    return x * lax.rsqrt(ms + EPS) * g


def _sigmoid(x):
    return 1.0 / (1.0 + jnp.exp2(x * (-LOG2_E)))


def _layernorm_swish(y, g, b):
    mu = jnp.mean(y, axis=-1, keepdims=True)
    yc = y - mu
    var = jnp.mean(yc * yc, axis=-1, keepdims=True)
    z = yc * lax.rsqrt(var + EPS) * g + b
    return z * _sigmoid(z)


VEC_ROWS = {"nmix": (0, 1), "nmem": (1, 1), "nffn": (2, 1), "nfin": (3, 1), "cbb": (4, 1),
            "lng": (5, 1), "lnb": (6, 1), "bgate": (8, 3), "caw": (11, K_A), "cbw": (16, K_B)}
VEC_TABLE_ROWS = 48


def _pack_vectors(d, **named):
    assert set(named) == set(VEC_ROWS)
    pieces, pos = [], 0
    for name, (first, rows) in sorted(VEC_ROWS.items(), key=lambda kv: kv[1]):
        pieces += [jnp.zeros((first - pos, d), F32), named[name].reshape(rows, d)]
        pos = first + rows
    pieces.append(jnp.zeros((VEC_TABLE_ROWS - pos, d), F32))
    return jnp.concatenate([p for p in pieces if p.shape[0]], axis=0)


class _vec:
    def __init__(self, vec_ref, name):
        self.ref = vec_ref
        self.first, self.rows = VEC_ROWS[name]

    def __getitem__(self, idx):
        rows, cols = (slice(None), slice(None)) if idx is Ellipsis else idx
        start = rows.start or 0
        stop = self.rows if rows.stop is None else rows.stop
        return self.ref[self.first + start:self.first + stop, cols]


def _resident(shape):
    nd = len(shape)
    return pl.BlockSpec(shape, lambda *_: (0,) * nd, pipeline_mode=pl.Buffered(1))


def _to_tiles(a, lanes):
    rows, dx = a.shape
    dh = dx // N_HEADS
    blocks = [a[:, hd * dh + half * lanes:hd * dh + (half + 1) * lanes]
              for half in range(dh // lanes) for hd in range(N_HEADS)]
    return jnp.concatenate(blocks, axis=-1).reshape(rows, dx // lanes, lanes)


def _prep_kernel(mem_ref, vec_ref, wk_ref, wv_ref, xs_ref, win_ref, *rest,
                 n_cast, n_batch, q_block):
    cast_in = rest[:n_cast]
    (k_ref, v_ref, kt_ref, vb_ref, proj_ref, winb_ref, q_ref, stack_ref,
     wk_buf, wv_buf) = rest[n_cast:]
    g_ref, nmix_ref = _vec(vec_ref, "nmem"), _vec(vec_ref, "nmix")
    j = pl.program_id(0)
    lanes = q_ref.shape[-1]

    @pl.when(j == 0)
    def _():
        wk_buf[...] = _from_tiles(wk_ref[...]).astype(BF16)
        wv_buf[...] = _from_tiles(wv_ref[...]).astype(BF16)

    xn = _rms(xs_ref[...], nmix_ref[...]).astype(BF16)
    wb = win_ref[...].astype(BF16)
    winb_ref[...] = wb
    proj = _dot(xn, wb)
    proj_ref[...] = proj

    @pl.when(j == q_block)
    def _():
        q_ref[...] = _to_tiles(proj, lanes)

    @pl.when(j < n_batch)
    def _():
        for i, src in enumerate(cast_in):
            stack_ref[i] = src[...].astype(BF16)
        mn = _rms(mem_ref[0], g_ref[...]).astype(BF16)
        k = _dot(mn, wk_buf[...])
        v = _dot(mn, wv_buf[...])
        vb_ref[0] = v.astype(BF16)
        kt_ref[0] = k.T.astype(BF16)
        k_ref[0] = _to_tiles(k, lanes)
        v_ref[0] = _to_tiles(v, lanes)


def _prep_call(mem, vecs, wk, wv, xs, w_in, cast_weights, q_block):
    b, n_mem, d = mem.shape
    m = xs.shape[0]
    dx = wk.shape[1] * wk.shape[2]
    d_in = w_in.shape[1]
    steps = d_in // d
    assert b <= steps
    cs = dx // LANES
    batch = lambda j: jnp.minimum(j, b - 1)
    chunk_spec = pl.BlockSpec((1, n_mem, cs, LANES), lambda j: (batch(j), 0, 0, 0))
    bf16_rows = 2 * SUBLANES
    rows, cols = cast_weights[0].shape
    assert all(w.shape == (rows, cols) for w in cast_weights) and rows % (b * bf16_rows) == 0
    n_cast = len(cast_weights)
    cast_specs = [pl.BlockSpec((rows // b, cols), lambda j: (batch(j), 0))] * n_cast
    outs = pl.pallas_call(
        functools.partial(_prep_kernel, n_cast=len(cast_weights), n_batch=b, q_block=q_block),
        grid=(steps,),
        in_specs=[
            pl.BlockSpec((1, n_mem, d), lambda j: (batch(j), 0, 0)),
            _resident(vecs.shape),
            _resident(wk.shape),
            _resident(wv.shape),
            _resident((m, d)),
            pl.BlockSpec((d, d), lambda j: (0, j)),
        ] + cast_specs,
        out_specs=[
            chunk_spec, chunk_spec,
            pl.BlockSpec((1, dx, n_mem), lambda j: (batch(j), 0, 0)),
            pl.BlockSpec((1, n_mem, dx), lambda j: (batch(j), 0, 0)),
            pl.BlockSpec((m, d), lambda j: (0, j)),
            pl.BlockSpec((d, d), lambda j: (0, j)),
            pl.BlockSpec((m, cs, LANES), lambda j: (0, 0, 0)),
            pl.BlockSpec((n_cast, rows // b, cols), lambda j: (0, batch(j), 0)),
        ],
        out_shape=[
            jax.ShapeDtypeStruct((b, n_mem, cs, LANES), F32),
            jax.ShapeDtypeStruct((b, n_mem, cs, LANES), F32),
            jax.ShapeDtypeStruct((b, dx, n_mem), BF16),
            jax.ShapeDtypeStruct((b, n_mem, dx), BF16),
            jax.ShapeDtypeStruct((m, d_in), F32),
            jax.ShapeDtypeStruct((d, d_in), BF16),
            jax.ShapeDtypeStruct((m, cs, LANES), F32),
            jax.ShapeDtypeStruct((n_cast, rows, cols), BF16),
        ],
        scratch_shapes=[pltpu.VMEM((d, dx), BF16), pltpu.VMEM((d, dx), BF16)],
        compiler_params=pltpu.CompilerParams(
            dimension_semantics=("arbitrary",), vmem_limit_bytes=48 * MIB),
        name="prep",
    )(mem, vecs, wk, wv, xs, w_in, *cast_weights)
    return outs[:4], outs[4:7], outs[7]


def _mixer_kernel(x_ref, kt_ref, vb_ref, w_in_ref, wout_ref,
                  vec_ref, *rest, tm, d, n_cast, state_tokens):
    nmix_ref, bgate_ref, caw_ref, cbw_ref, cbb_ref, lng_ref, lnb_ref = (
        _vec(vec_ref, name) for name in ("nmix", "bgate", "caw", "cbw", "cbb", "lng", "lnb"))
    rest = list(rest)
    cast_in = [rest.pop(0) for _ in range(n_cast)]
    state_in = [rest.pop(0) for _ in range(3)]
    h_ref, na_ref, nb_ref = [rest.pop(0) for _ in range(3)]
    cast_out = [rest.pop(0) for _ in range(n_cast)]
    state_out = [rest.pop(0) for _ in range(4)]
    ua_buf, ub_buf, convb_buf, preb_buf = rest
    t = pl.program_id(1)
    dh = d // N_HEADS

    @pl.when(t == 0)
    def _():
        ua_buf[0:HALO_A, :] = jnp.zeros((HALO_A, d), F32)
        ub_buf[0:HALO_B, :] = jnp.zeros((HALO_B, d), F32)

    x = x_ref[0]
    xn = _rms(x, nmix_ref[...]).astype(BF16)

    def proj(c0, c1):
        return _dot(xn, w_in_ref[:, c0 * d:c1 * d])

    def gate(i):
        return _sigmoid(proj(6 + i, 7 + i) + bgate_ref[i:i + 1, :])

    def glu():
        ub_buf[HALO_B:HALO_B + tm, :] = proj(3, 4) * _sigmoid(proj(4, 5))

    def conv_block(c, lb):
        r0 = c * CONV_ROWS
        win_rows = CONV_ROWS + HALO_B
        lanes = slice(lb * LANES, (lb + 1) * LANES)
        win = ub_buf[r0:r0 + win_rows, lanes]
        acc = jnp.zeros((CONV_ROWS, LANES), F32)
        for r in range(SUBLANES):
            rolled = win if r == 0 else pltpu.roll(win, win_rows - r, 0)
            for a in range(win_rows // SUBLANES):
                j = SUBLANES * a + r - (HALO_B - (K_B - 1))
                if 0 <= j < K_B:
                    rows = rolled[SUBLANES * a:SUBLANES * a + CONV_ROWS, :]
                    acc = acc + cbw_ref[j:j + 1, lanes] * rows
        convb_buf[r0:r0 + CONV_ROWS, lanes] = acc + cbb_ref[:, lanes]

    def norm_chunk(c):
        r0 = c * CONV_ROWS
        z = _layernorm_swish(convb_buf[r0:r0 + CONV_ROWS, :], lng_ref[...], lnb_ref[...])
        preb_buf[r0:r0 + CONV_ROWS, :] = z.astype(BF16)

    out = {}

    def branch_a():
        cx = proj(1, 3)
        ua_buf[HALO_A:HALO_A + tm, :] = cx[:, :d] * cx[:, d:]
        conv_a = caw_ref[0:1, :] * ua_buf[HALO_A - 2:HALO_A - 2 + tm, :]
        conv_a = conv_a + caw_ref[1:2, :] * ua_buf[HALO_A - 1:HALO_A - 1 + tm, :]
        conv_a = conv_a + caw_ref[2:3, :] * ua_buf[HALO_A:HALO_A + tm, :]
        out["pre_a"] = (proj(0, 1) * conv_a).astype(BF16)
        na_ref[0] = ua_buf[HALO_A + tm - (K_A - 1):HALO_A + tm, :]
        ua_buf[0:HALO_A, :] = ua_buf[tm:tm + HALO_A, :]

    def branch_a_out():
        out["ya"] = gate(0) * _dot(out["pre_a"], wout_ref[0])

    def query():
        out["q"] = (proj(5, 6) * (dh ** -0.5)).astype(BF16)
        out["heads"] = []

    def head(hd):
        sl = slice(hd * dh, (hd + 1) * dh)
        s = _dot(out["q"][:, sl], kt_ref[0, sl, :])
        e = jnp.exp(s - jnp.max(s, axis=-1, keepdims=True))
        p = e * (1.0 / jnp.sum(e, axis=-1, keepdims=True))
        out["heads"].append(_dot(p.astype(BF16), vb_ref[0, :, sl]))

    def branch_x_out():
        o = jnp.concatenate(out["heads"], axis=-1).astype(BF16)
        out["yx"] = gate(2) * _dot(o, wout_ref[2])

    def gate_b():
        out["gb"] = gate(1)

    mxu_work = [branch_a, branch_a_out, query,
                functools.partial(head, 0), functools.partial(head, 1),
                functools.partial(head, 2), functools.partial(head, 3),
                branch_x_out, gate_b]
    n_chunks = tm // CONV_ROWS
    vector_work = [glu]
    for c in range(n_chunks):
        for lb in range(d // LANES):
            vector_work.append(functools.partial(conv_block, c, lb))
        vector_work.append(functools.partial(norm_chunk, c))
    done = 0
    for i, work in enumerate(vector_work):
        work()
        while done < (i + 1) * len(mxu_work) // len(vector_work):
            mxu_work[done]()
            done += 1

    nb_ref[0] = ub_buf[HALO_B + tm - (K_B - 1):HALO_B + tm, :]
    ub_buf[0:HALO_B, :] = ub_buf[tm:tm + HALO_B, :]

    yb = _dot(preb_buf[...], wout_ref[1])
    merged = out["ya"] + out["gb"] * yb + out["yx"]
    h_ref[0] = x + _dot(merged.astype(BF16), wout_ref[3])

    for src, dst in zip(cast_in, cast_out):
        dst[...] = src[...].astype(BF16)
    _sample_state_update(*state_in, caw_ref, cbw_ref, cbb_ref, lng_ref, lnb_ref,
                         *state_out, state_tokens, d)


def _mixer_call(x, kt, vb, w_in, wout, vecs, cast_weights, sample_state):
    b, t, d = x.shape
    tm = PROMPT_TILE
    nt = t // tm
    steps = b * nt
    n_mem = vb.shape[1]
    d_in = w_in.shape[1]
    step = lambda i, j: i * nt + j
    bf16_rows = 2 * SUBLANES
    cast_specs = []
    for w in cast_weights:
        rows, cols = w.shape
        chunks = steps
        while rows % (chunks * bf16_rows):
            chunks //= 2
        cast_specs.append(pl.BlockSpec(
            (rows // chunks, cols),
            lambda i, j, chunks=chunks: (jnp.minimum(step(i, j), chunks - 1), 0)))

    proj_s, sa, sb = sample_state
    m = proj_s.shape[0]
    tb = SAMPLE_TOKENS
    assert m % tb == 0 and m // tb <= steps
    tok = lambda i, j: jnp.minimum(step(i, j), m // tb - 1)
    row_spec = pl.BlockSpec((tb, d), lambda i, j: (tok(i, j), 0))
    state_a_spec = pl.BlockSpec((tb, K_A - 1, d), lambda i, j: (tok(i, j), 0, 0))
    state_b_spec = pl.BlockSpec((K_B - 1, tb, d), lambda i, j: (0, tok(i, j), 0))
    state_in_specs = [pl.BlockSpec((tb, 5 * d), lambda i, j: (tok(i, j), 0)),
                      state_a_spec, state_b_spec]
    state_out_specs = [row_spec, row_spec, state_a_spec, state_b_spec]
    state_out_shape = [jax.ShapeDtypeStruct((m, d), F32), jax.ShapeDtypeStruct((m, d), F32),
                       jax.ShapeDtypeStruct(sa.shape, F32), jax.ShapeDtypeStruct(sb.shape, F32)]

    kernel = functools.partial(_mixer_kernel, tm=tm, d=d, n_cast=len(cast_weights),
                               state_tokens=tb)
    outs = pl.pallas_call(
        kernel,
        grid=(b, nt),
        in_specs=[
            pl.BlockSpec((1, tm, d), lambda i, j: (i, j, 0)),
            pl.BlockSpec((1, d, n_mem), lambda i, j: (i, 0, 0)),
            pl.BlockSpec((1, n_mem, d), lambda i, j: (i, 0, 0)),
            _resident((d, d_in)),
            _resident(wout.shape),
            _resident(vecs.shape),
        ] + cast_specs + state_in_specs,
        out_specs=[
            pl.BlockSpec((1, tm, d), lambda i, j: (i, j, 0)),
            pl.BlockSpec((1, K_A - 1, d), lambda i, j: (i, 0, 0)),
            pl.BlockSpec((1, K_B - 1, d), lambda i, j: (i, 0, 0)),
        ] + cast_specs + state_out_specs,
        out_shape=[
            jax.ShapeDtypeStruct((b, t, d), F32),
            jax.ShapeDtypeStruct((b, K_A - 1, d), F32),
            jax.ShapeDtypeStruct((b, K_B - 1, d), F32),
        ] + [jax.ShapeDtypeStruct(w.shape, BF16) for w in cast_weights] + state_out_shape,
        scratch_shapes=[
            pltpu.VMEM((HALO_A + tm, d), F32),
            pltpu.VMEM((HALO_B + tm, d), F32),
            pltpu.VMEM((tm, d), F32),
            pltpu.VMEM((tm, d), BF16),
        ],
        compiler_params=pltpu.CompilerParams(
            dimension_semantics=("arbitrary", "arbitrary"), vmem_limit_bytes=62 * MIB),
        name="mixer",
    )(x, kt, vb, w_in, wout, vecs, *cast_weights, proj_s, sa, sb)
    n_cast = len(cast_weights)
    return outs[:3], outs[3:3 + n_cast], outs[3 + n_cast:]


def _token_attention(q, k, v):
    n_mem, cs, lanes = k.shape
    scale = (cs // N_HEADS * lanes) ** -0.5
    s = jnp.sum(k * (q * scale)[None], axis=-1, keepdims=True)
    s = s + pltpu.roll(s, N_HEADS, 1)
    e = jnp.exp(s - jnp.max(s, axis=0, keepdims=True))
    den = jnp.sum(e, axis=0)
    return jnp.sum(e * v, axis=0) * (1.0 / den)


def _ffn_kernel(h_ref, vec_ref, wg_ref, wu_ref, wd_ref, *rest, attn_tokens):
    nffn_ref, nfin_ref = _vec(vec_ref, "nffn"), _vec(vec_ref, "nfin")
    if attn_tokens:
        qv_ref, k_ref, v_ref, y_ref, o_ref = rest
        for i in range(attn_tokens):
            o_ref[i] = _token_attention(qv_ref[i], k_ref[i], v_ref[i])
    else:
        (y_ref,) = rest
    h = h_ref[...]
    hn = _rms(h, nffn_ref[...]).astype(BF16)
    gate = _dot(hn, wg_ref[...])
    up = _dot(hn, wu_ref[...])
    act = (gate * _sigmoid(gate) * up).astype(BF16)
    h2 = h + _dot(act, wd_ref[...])
    y = _rms(h2, nfin_ref[...])
    if len(y_ref.shape) == 3:
        y_ref[:, 0, :] = y
    else:
        y_ref[...] = y


def _ffn_call(h, vecs, wg, wu, wd, tm, attn=None, per_sequence_rows=False):
    m, d = h.shape
    dff = wg.shape[1]
    steps = m // tm
    in_specs = [
        pl.BlockSpec((tm, d), lambda i: (i, 0)),
        _resident(vecs.shape),
        _resident((d, dff)), _resident((d, dff)), _resident((dff, d)),
    ]
    if per_sequence_rows:
        out_specs = [pl.BlockSpec((tm, 1, d), lambda i: (i, 0, 0))]
        out_shape = [jax.ShapeDtypeStruct((m, 1, d), F32)]
    else:
        out_specs = [pl.BlockSpec((tm, d), lambda i: (i, 0))]
        out_shape = [jax.ShapeDtypeStruct((m, d), F32)]
    args = [h, vecs, wg, wu, wd]
    attn_tokens = 0
    if attn is not None:
        qv, k, v = attn
        n_tok, n_mem, cs, lanes = k.shape
        assert n_tok % steps == 0
        attn_tokens = n_tok // steps
        chunk_spec = pl.BlockSpec((attn_tokens, cs, lanes), lambda i: (i, 0, 0))
        cache_spec = pl.BlockSpec((attn_tokens, n_mem, cs, lanes), lambda i: (i, 0, 0, 0))
        in_specs += [chunk_spec, cache_spec, cache_spec]
        out_specs.append(chunk_spec)
        out_shape.append(jax.ShapeDtypeStruct((n_tok, cs, lanes), F32))
        args += [qv, k, v]
    return pl.pallas_call(
        functools.partial(_ffn_kernel, attn_tokens=attn_tokens),
        grid=(steps,),
        in_specs=in_specs,
        out_specs=out_specs,
        out_shape=out_shape,
        compiler_params=pltpu.CompilerParams(
            dimension_semantics=("arbitrary",), vmem_limit_bytes=56 * MIB),
        name="ffn_attn" if attn_tokens else "ffn",
    )(*args)


def _sample_state_update(pr_ref, sa_ref, sb_ref, caw_ref, cbw_ref, cbb_ref,
                         lng_ref, lnb_ref, prea_ref, preb_ref, na_ref, nb_ref, tb, d):
    a_b = pr_ref[:, 0:d]
    ua = pr_ref[:, d:2 * d] * pr_ref[:, 2 * d:3 * d]
    ub = pr_ref[:, 3 * d:4 * d] * _sigmoid(pr_ref[:, 4 * d:5 * d])

    conv_b = cbw_ref[K_B - 1:K_B, :] * ub + cbb_ref[...]
    for j in range(K_B - 1):
        conv_b = conv_b + cbw_ref[j:j + 1, :] * sb_ref[j]
    preb_ref[...] = _layernorm_swish(conv_b, lng_ref[...], lnb_ref[...])
    for j in range(K_B - 2):
        nb_ref[j] = sb_ref[j + 1]
    nb_ref[K_B - 2] = ub

    for i in range(tb):
        row = slice(i, i + 1)
        conv_a = (caw_ref[0:1, :] * sa_ref[i, 0:1, :] + caw_ref[1:2, :] * sa_ref[i, 1:2, :]
                  + caw_ref[2:3, :] * ua[row])
        prea_ref[row, :] = a_b[row] * conv_a
        na_ref[i, 0:1, :] = sa_ref[i, 1:2, :]
        na_ref[i, 1:2, :] = ua[row]


def _to_chunks(a, lanes):
    *lead, nh, dh = a.shape
    a = a.reshape(*lead, nh, dh // lanes, lanes)
    a = jnp.swapaxes(a, -3, -2)
    return a.reshape(*lead, (dh // lanes) * nh, lanes)


def _from_chunks(a, nh):
    *lead, cs, lanes = a.shape
    a = a.reshape(*lead, cs // nh, nh, lanes)
    a = jnp.swapaxes(a, -3, -2)
    return a.reshape(*lead, nh, (cs // nh) * lanes)


def _from_tiles(a):
    rows, cs, lanes = a.shape
    flat = a.reshape(rows, cs * lanes)
    blocks = [flat[:, (half * N_HEADS + hd) * lanes:(half * N_HEADS + hd + 1) * lanes]
              for hd in range(N_HEADS) for half in range(cs // N_HEADS)]
    return jnp.concatenate(blocks, axis=-1)


def _spost_kernel(x_ref, prea_ref, preb_ref, o_ref, gl_ref, vec_ref,
                  wout_ref, h_ref, *, d):
    bgate_ref = _vec(vec_ref, "bgate")

    def gate(i):
        return _sigmoid(gl_ref[:, i * d:(i + 1) * d] + bgate_ref[i:i + 1, :])

    ya = _dot(prea_ref[...].astype(BF16), wout_ref[0])
    yb = _dot(preb_ref[...].astype(BF16), wout_ref[1])
    yx = _dot(_from_tiles(o_ref[...]).astype(BF16), wout_ref[2])
    merged = gate(0) * ya + gate(1) * yb + gate(2) * yx
    h_ref[...] = x_ref[...] + _dot(merged.astype(BF16), wout_ref[3])


def _spost_call(x, prea, preb, o_tiles, proj, vecs, wout):
    m, d = x.shape
    kernel = functools.partial(_spost_kernel, d=d)
    full = pl.BlockSpec((m, d), lambda i: (0, 0))
    return pl.pallas_call(
        kernel,
        grid=(1,),
        in_specs=[
            full, full, full,
            pl.BlockSpec(o_tiles.shape, lambda i: (0, 0, 0)),
            pl.BlockSpec((m, 3 * d), lambda i: (0, 2)),
            _resident(vecs.shape),
            _resident(wout.shape),
        ],
        out_specs=full,
        out_shape=jax.ShapeDtypeStruct((m, d), F32),
        compiler_params=pltpu.CompilerParams(
            dimension_semantics=("arbitrary",), vmem_limit_bytes=32 * MIB),
        name="sample_post",
    )(x, prea, preb, o_tiles, proj, vecs, wout)


def kernel(x_prompt, x_sample, mem_prompt, cache_mem_k, cache_mem_v, state_conv_a, state_conv_b, norm_mix, w_in, b_gate, conv_a_w, w_out_a, conv_b_w, conv_b_bias, ln_b_g, ln_b_b, w_out_b, norm_mem, w_k, w_v, w_out_x, w_o, norm_ffn, w_ff_gate, w_ff_up, w_ff_down, norm_final):
    depth = w_in.shape[0]
    assert depth == 1, "single-layer step only"
    b, t, d = x_prompt.shape
    sb_, st_, _ = x_sample.shape
    assert st_ == 1
    n_mem = mem_prompt.shape[1]
    nh, dh = w_k.shape[2], w_k.shape[3]
    assert nh == N_HEADS and nh * dh == d
    assert dh == 2 * LANES and 2 * nh == SUBLANES

    wk, wv = _to_chunks(w_k[0], LANES), _to_chunks(w_v[0], LANES)
    vecs = _pack_vectors(
        d, nmix=norm_mix[0], nmem=norm_mem[0], nffn=norm_ffn[0], nfin=norm_final,
        cbb=conv_b_bias[0], lng=ln_b_g[0], lnb=ln_b_b[0], bgate=b_gate[0],
        caw=conv_a_w[0], cbw=conv_b_w[0])

    xs = x_sample.reshape(sb_, d)
    (k_p, v_p, kt, vb), (proj_s, w_in_b, q_s), wout = _prep_call(
        mem_prompt, vecs, wk, wv, xs, w_in[0],
        [w_out_a[0], w_out_b[0], w_out_x[0], w_o[0]], q_block=5)

    (h_p, ca_p, cb_p), (wg, wu, wd), (prea, preb, ca_s, cb_s) = _mixer_call(
        x_prompt, kt, vb, w_in_b, wout, vecs,
        [w_ff_gate[0], w_ff_up[0], w_ff_down[0]],
        (proj_s, state_conv_a[0], jnp.swapaxes(state_conv_b[0], 0, 1)))

    y_p, o_s = _ffn_call(
        h_p.reshape(b * t, d), vecs, wg, wu, wd, PROMPT_TILE,
        attn=(q_s, _to_chunks(cache_mem_k[0], LANES), _to_chunks(cache_mem_v[0], LANES)))

    h_s = _spost_call(xs, prea, preb, o_s, proj_s, vecs, wout)
    (y_s,) = _ffn_call(h_s, vecs, wg, wu, wd, sb_, per_sequence_rows=True)

    return (y_p.reshape(b, t, d), y_s,
            _from_chunks(k_p, nh)[None], _from_chunks(v_p, nh)[None],
            ca_p[None], cb_p[None], ca_s[None], jnp.swapaxes(cb_s, 0, 1)[None])
```

```python
import functools

import jax
import jax.numpy as jnp
from jax import lax
from jax.experimental import pallas as pl
from jax.experimental.pallas import tpu as pltpu

F32 = jnp.float32
BF16 = jnp.bfloat16
EPS = 1e-6
LOG2_E = 1.4426950408889634

K_A = 3
K_B = 31
N_HEADS = 4
SUBLANES = 8
LANES = 128
HALO_A = 8
HALO_B = 32
CONV_ROWS = 64

PROMPT_TILE = 512
SAMPLE_TOKENS = 8
MIB = 1024 * 1024


def _dot(a, b):
    return jnp.dot(a, b, preferred_element_type=F32)


def _rms(x, g):
    ms = jnp.mean(x * x, axis=-1, keepdims=True)
    return x * lax.rsqrt(ms + EPS) * g


def _sigmoid(x):
    return 1.0 / (1.0 + jnp.exp2(x * (-LOG2_E)))


def _layernorm_swish(y, g, b):
    mu = jnp.mean(y, axis=-1, keepdims=True)
    yc = y - mu
    var = jnp.mean(yc * yc, axis=-1, keepdims=True)
    z = yc * lax.rsqrt(var + EPS) * g + b
    return z * _sigmoid(z)


VEC_ROWS = {"nmix": (0, 1), "nmem": (1, 1), "nffn": (2, 1), "nfin": (3, 1), "cbb": (4, 1),
            "lng": (5, 1), "lnb": (6, 1), "bgate": (8, 3), "caw": (11, K_A), "cbw": (16, K_B)}
VEC_TABLE_ROWS = 48


def _pack_vectors(d, **named):
    assert set(named) == set(VEC_ROWS)
    pieces, pos = [], 0
    for name, (first, rows) in sorted(VEC_ROWS.items(), key=lambda kv: kv[1]):
        pieces += [jnp.zeros((first - pos, d), F32), named[name].reshape(rows, d)]
        pos = first + rows
    pieces.append(jnp.zeros((VEC_TABLE_ROWS - pos, d), F32))
    return jnp.concatenate([p for p in pieces if p.shape[0]], axis=0)


class _vec:
    def __init__(self, vec_ref, name):
        self.ref = vec_ref
        self.first, self.rows = VEC_ROWS[name]

    def __getitem__(self, idx):
        rows, cols = (slice(None), slice(None)) if idx is Ellipsis else idx
        start = rows.start or 0
        stop = self.rows if rows.stop is None else rows.stop
        return self.ref[self.first + start:self.first + stop, cols]


def _resident(shape):
    nd = len(shape)
    return pl.BlockSpec(shape, lambda *_: (0,) * nd, pipeline_mode=pl.Buffered(1))


def _to_tiles(a, lanes):
    rows, dx = a.shape
    dh = dx // N_HEADS
    blocks = [a[:, hd * dh + half * lanes:hd * dh + (half + 1) * lanes]
              for half in range(dh // lanes) for hd in range(N_HEADS)]
    return jnp.concatenate(blocks, axis=-1).reshape(rows, dx // lanes, lanes)


def _prep_kernel(mem_ref, vec_ref, wk_ref, wv_ref, xs_ref, win_ref, *rest,
                 n_cast, n_batch, q_block):
    cast_in = rest[:n_cast]
    (k_ref, v_ref, kt_ref, vb_ref, proj_ref, winb_ref, q_ref, stack_ref,
     wk_buf, wv_buf) = rest[n_cast:]
    g_ref, nmix_ref = _vec(vec_ref, "nmem"), _vec(vec_ref, "nmix")
    j = pl.program_id(0)
    lanes = q_ref.shape[-1]

    @pl.when(j == 0)
    def _():
        wk_buf[...] = _from_tiles(wk_ref[...]).astype(BF16)
        wv_buf[...] = _from_tiles(wv_ref[...]).astype(BF16)

    xn = _rms(xs_ref[...], nmix_ref[...]).astype(BF16)
    wb = win_ref[...].astype(BF16)
    winb_ref[...] = wb
    proj = _dot(xn, wb)
    proj_ref[...] = proj

    @pl.when(j == q_block)
    def _():
        q_ref[...] = _to_tiles(proj, lanes)

    @pl.when(j < n_batch)
    def _():
        for i, src in enumerate(cast_in):
            stack_ref[i] = src[...].astype(BF16)
        mn = _rms(mem_ref[0], g_ref[...]).astype(BF16)
        k = _dot(mn, wk_buf[...])
        v = _dot(mn, wv_buf[...])
        vb_ref[0] = v.astype(BF16)
        kt_ref[0] = k.T.astype(BF16)
        k_ref[0] = _to_tiles(k, lanes)
        v_ref[0] = _to_tiles(v, lanes)


def _prep_call(mem, vecs, wk, wv, xs, w_in, cast_weights, q_block):
    b, n_mem, d = mem.shape
    m = xs.shape[0]
    dx = wk.shape[1] * wk.shape[2]
    d_in = w_in.shape[1]
    steps = d_in // d
    assert b <= steps
    cs = dx // LANES
    batch = lambda j: jnp.minimum(j, b - 1)
    chunk_spec = pl.BlockSpec((1, n_mem, cs, LANES), lambda j: (batch(j), 0, 0, 0))
    bf16_rows = 2 * SUBLANES
    rows, cols = cast_weights[0].shape
    assert all(w.shape == (rows, cols) for w in cast_weights) and rows % (b * bf16_rows) == 0
    n_cast = len(cast_weights)
    cast_specs = [pl.BlockSpec((rows // b, cols), lambda j: (batch(j), 0))] * n_cast
    outs = pl.pallas_call(
        functools.partial(_prep_kernel, n_cast=len(cast_weights), n_batch=b, q_block=q_block),
        grid=(steps,),
        in_specs=[
            pl.BlockSpec((1, n_mem, d), lambda j: (batch(j), 0, 0)),
            _resident(vecs.shape),
            _resident(wk.shape),
            _resident(wv.shape),
            _resident((m, d)),
            pl.BlockSpec((d, d), lambda j: (0, j)),
        ] + cast_specs,
        out_specs=[
            chunk_spec, chunk_spec,
            pl.BlockSpec((1, dx, n_mem), lambda j: (batch(j), 0, 0)),
            pl.BlockSpec((1, n_mem, dx), lambda j: (batch(j), 0, 0)),
            pl.BlockSpec((m, d), lambda j: (0, j)),
            pl.BlockSpec((d, d), lambda j: (0, j)),
            pl.BlockSpec((m, cs, LANES), lambda j: (0, 0, 0)),
            pl.BlockSpec((n_cast, rows // b, cols), lambda j: (0, batch(j), 0)),
        ],
        out_shape=[
            jax.ShapeDtypeStruct((b, n_mem, cs, LANES), F32),
            jax.ShapeDtypeStruct((b, n_mem, cs, LANES), F32),
            jax.ShapeDtypeStruct((b, dx, n_mem), BF16),
            jax.ShapeDtypeStruct((b, n_mem, dx), BF16),
            jax.ShapeDtypeStruct((m, d_in), F32),
            jax.ShapeDtypeStruct((d, d_in), BF16),
            jax.ShapeDtypeStruct((m, cs, LANES), F32),
            jax.ShapeDtypeStruct((n_cast, rows, cols), BF16),
        ],
        scratch_shapes=[pltpu.VMEM((d, dx), BF16), pltpu.VMEM((d, dx), BF16)],
        compiler_params=pltpu.CompilerParams(
            dimension_semantics=("arbitrary",), vmem_limit_bytes=48 * MIB),
        name="prep",
    )(mem, vecs, wk, wv, xs, w_in, *cast_weights)
    return outs[:4], outs[4:7], outs[7]


def _mixer_kernel(x_ref, kt_ref, vb_ref, w_in_ref, wout_ref,
                  vec_ref, *rest, tm, d, n_cast, state_tokens):
    nmix_ref, bgate_ref, caw_ref, cbw_ref, cbb_ref, lng_ref, lnb_ref = (
        _vec(vec_ref, name) for name in ("nmix", "bgate", "caw", "cbw", "cbb", "lng", "lnb"))
    rest = list(rest)
    cast_in = [rest.pop(0) for _ in range(n_cast)]
    state_in = [rest.pop(0) for _ in range(3)]
    h_ref, na_ref, nb_ref = [rest.pop(0) for _ in range(3)]
    cast_out = [rest.pop(0) for _ in range(n_cast)]
    state_out = [rest.pop(0) for _ in range(4)]
    ua_buf, ub_buf, convb_buf, preb_buf = rest
    t = pl.program_id(1)
    dh = d // N_HEADS

    @pl.when(t == 0)
    def _():
        ua_buf[0:HALO_A, :] = jnp.zeros((HALO_A, d), F32)
        ub_buf[0:HALO_B, :] = jnp.zeros((HALO_B, d), F32)

    x = x_ref[0]
    xn = _rms(x, nmix_ref[...]).astype(BF16)

    def proj(c0, c1):
        return _dot(xn, w_in_ref[:, c0 * d:c1 * d])

    def gate(i):
        return _sigmoid(proj(6 + i, 7 + i) + bgate_ref[i:i + 1, :])

    def glu():
        ub_buf[HALO_B:HALO_B + tm, :] = proj(3, 4) * _sigmoid(proj(4, 5))

    def conv_block(c, lb):
        r0 = c * CONV_ROWS
        win_rows = CONV_ROWS + HALO_B
        lanes = slice(lb * LANES, (lb + 1) * LANES)
        win = ub_buf[r0:r0 + win_rows, lanes]
        acc = jnp.zeros((CONV_ROWS, LANES), F32)
        for r in range(SUBLANES):
            rolled = win if r == 0 else pltpu.roll(win, win_rows - r, 0)
            for a in range(win_rows // SUBLANES):
                j = SUBLANES * a + r - (HALO_B - (K_B - 1))
                if 0 <= j < K_B:
                    rows = rolled[SUBLANES * a:SUBLANES * a + CONV_ROWS, :]
                    acc = acc + cbw_ref[j:j + 1, lanes] * rows
        convb_buf[r0:r0 + CONV_ROWS, lanes] = acc + cbb_ref[:, lanes]

    def norm_chunk(c):
        r0 = c * CONV_ROWS
        z = _layernorm_swish(convb_buf[r0:r0 + CONV_ROWS, :], lng_ref[...], lnb_ref[...])
        preb_buf[r0:r0 + CONV_ROWS, :] = z.astype(BF16)

    out = {}

    def branch_a():
        cx = proj(1, 3)
        ua_buf[HALO_A:HALO_A + tm, :] = cx[:, :d] * cx[:, d:]
        conv_a = caw_ref[0:1, :] * ua_buf[HALO_A - 2:HALO_A - 2 + tm, :]
        conv_a = conv_a + caw_ref[1:2, :] * ua_buf[HALO_A - 1:HALO_A - 1 + tm, :]
        conv_a = conv_a + caw_ref[2:3, :] * ua_buf[HALO_A:HALO_A + tm, :]
        out["pre_a"] = (proj(0, 1) * conv_a).astype(BF16)
        na_ref[0] = ua_buf[HALO_A + tm - (K_A - 1):HALO_A + tm, :]
        ua_buf[0:HALO_A, :] = ua_buf[tm:tm + HALO_A, :]

    def branch_a_out():
        out["ya"] = gate(0) * _dot(out["pre_a"], wout_ref[0])

    def query():
        out["q"] = (proj(5, 6) * (dh ** -0.5)).astype(BF16)
        out["heads"] = []

    def head(hd):
        sl = slice(hd * dh, (hd + 1) * dh)
        s = _dot(out["q"][:, sl], kt_ref[0, sl, :])
        e = jnp.exp(s - jnp.max(s, axis=-1, keepdims=True))
        p = e * (1.0 / jnp.sum(e, axis=-1, keepdims=True))
        out["heads"].append(_dot(p.astype(BF16), vb_ref[0, :, sl]))

    def branch_x_out():
        o = jnp.concatenate(out["heads"], axis=-1).astype(BF16)
        out["yx"] = gate(2) * _dot(o, wout_ref[2])

    def gate_b():
        out["gb"] = gate(1)

    mxu_work = [branch_a, branch_a_out, query,
                functools.partial(head, 0), functools.partial(head, 1),
                functools.partial(head, 2), functools.partial(head, 3),
                branch_x_out, gate_b]
    n_chunks = tm // CONV_ROWS
    vector_work = [glu]
    for c in range(n_chunks):
        for lb in range(d // LANES):
            vector_work.append(functools.partial(conv_block, c, lb))
        vector_work.append(functools.partial(norm_chunk, c))
    done = 0
    for i, work in enumerate(vector_work):
        work()
        while done < (i + 1) * len(mxu_work) // len(vector_work):
            mxu_work[done]()
            done += 1

    nb_ref[0] = ub_buf[HALO_B + tm - (K_B - 1):HALO_B + tm, :]
    ub_buf[0:HALO_B, :] = ub_buf[tm:tm + HALO_B, :]

    yb = _dot(preb_buf[...], wout_ref[1])
    merged = out["ya"] + out["gb"] * yb + out["yx"]
    h_ref[0] = x + _dot(merged.astype(BF16), wout_ref[3])

    for src, dst in zip(cast_in, cast_out):
        dst[...] = src[...].astype(BF16)
    _sample_state_update(*state_in, caw_ref, cbw_ref, cbb_ref, lng_ref, lnb_ref,
                         *state_out, state_tokens, d)


def _mixer_call(x, kt, vb, w_in, wout, vecs, cast_weights, sample_state):
    b, t, d = x.shape
    tm = PROMPT_TILE
    nt = t // tm
    steps = b * nt
    n_mem = vb.shape[1]
    d_in = w_in.shape[1]
    step = lambda i, j: i * nt + j
    bf16_rows = 2 * SUBLANES
    cast_specs = []
    for w in cast_weights:
        rows, cols = w.shape
        chunks = steps
        while rows % (chunks * bf16_rows):
            chunks //= 2
        cast_specs.append(pl.BlockSpec(
            (rows // chunks, cols),
            lambda i, j, chunks=chunks: (jnp.minimum(step(i, j), chunks - 1), 0)))

    proj_s, sa, sb = sample_state
    m = proj_s.shape[0]
    tb = SAMPLE_TOKENS
    assert m % tb == 0 and m // tb <= steps
    tok = lambda i, j: jnp.minimum(step(i, j), m // tb - 1)
    row_spec = pl.BlockSpec((tb, d), lambda i, j: (tok(i, j), 0))
    state_a_spec = pl.BlockSpec((tb, K_A - 1, d), lambda i, j: (tok(i, j), 0, 0))
    state_b_spec = pl.BlockSpec((K_B - 1, tb, d), lambda i, j: (0, tok(i, j), 0))
    state_in_specs = [pl.BlockSpec((tb, 5 * d), lambda i, j: (tok(i, j), 0)),
                      state_a_spec, state_b_spec]
    state_out_specs = [row_spec, row_spec, state_a_spec, state_b_spec]
    state_out_shape = [jax.ShapeDtypeStruct((m, d), F32), jax.ShapeDtypeStruct((m, d), F32),
                       jax.ShapeDtypeStruct(sa.shape, F32), jax.ShapeDtypeStruct(sb.shape, F32)]

    kernel = functools.partial(_mixer_kernel, tm=tm, d=d, n_cast=len(cast_weights),
                               state_tokens=tb)
    outs = pl.pallas_call(
        kernel,
        grid=(b, nt),
        in_specs=[
            pl.BlockSpec((1, tm, d), lambda i, j: (i, j, 0)),
            pl.BlockSpec((1, d, n_mem), lambda i, j: (i, 0, 0)),
            pl.BlockSpec((1, n_mem, d), lambda i, j: (i, 0, 0)),
            _resident((d, d_in)),
            _resident(wout.shape),
            _resident(vecs.shape),
        ] + cast_specs + state_in_specs,
        out_specs=[
            pl.BlockSpec((1, tm, d), lambda i, j: (i, j, 0)),
            pl.BlockSpec((1, K_A - 1, d), lambda i, j: (i, 0, 0)),
            pl.BlockSpec((1, K_B - 1, d), lambda i, j: (i, 0, 0)),
        ] + cast_specs + state_out_specs,
        out_shape=[
            jax.ShapeDtypeStruct((b, t, d), F32),
            jax.ShapeDtypeStruct((b, K_A - 1, d), F32),
            jax.ShapeDtypeStruct((b, K_B - 1, d), F32),
        ] + [jax.ShapeDtypeStruct(w.shape, BF16) for w in cast_weights] + state_out_shape,
        scratch_shapes=[
            pltpu.VMEM((HALO_A + tm, d), F32),
            pltpu.VMEM((HALO_B + tm, d), F32),
            pltpu.VMEM((tm, d), F32),
            pltpu.VMEM((tm, d), BF16),
        ],
        compiler_params=pltpu.CompilerParams(
            dimension_semantics=("arbitrary", "arbitrary"), vmem_limit_bytes=62 * MIB),
        name="mixer",
    )(x, kt, vb, w_in, wout, vecs, *cast_weights, proj_s, sa, sb)
    n_cast = len(cast_weights)
    return outs[:3], outs[3:3 + n_cast], outs[3 + n_cast:]


def _token_attention(q, k, v):
    n_mem, cs, lanes = k.shape
    scale = (cs // N_HEADS * lanes) ** -0.5
    s = jnp.sum(k * (q * scale)[None], axis=-1, keepdims=True)
    s = s + pltpu.roll(s, N_HEADS, 1)
    e = jnp.exp(s - jnp.max(s, axis=0, keepdims=True))
    den = jnp.sum(e, axis=0)
    return jnp.sum(e * v, axis=0) * (1.0 / den)


def _ffn_kernel(h_ref, vec_ref, wg_ref, wu_ref, wd_ref, *rest, attn_tokens):
    nffn_ref, nfin_ref = _vec(vec_ref, "nffn"), _vec(vec_ref, "nfin")
    if attn_tokens:
        qv_ref, k_ref, v_ref, y_ref, o_ref = rest
        for i in range(attn_tokens):
            o_ref[i] = _token_attention(qv_ref[i], k_ref[i], v_ref[i])
    else:
        (y_ref,) = rest
    h = h_ref[...]
    hn = _rms(h, nffn_ref[...]).astype(BF16)
    gate = _dot(hn, wg_ref[...])
    up = _dot(hn, wu_ref[...])
    act = (gate * _sigmoid(gate) * up).astype(BF16)
    h2 = h + _dot(act, wd_ref[...])
    y_ref[...] = _rms(h2, nfin_ref[...])


def _ffn_call(h, vecs, wg, wu, wd, tm, attn=None):
    m, d = h.shape
    dff = wg.shape[1]
    steps = m // tm
    in_specs = [
        pl.BlockSpec((tm, d), lambda i: (i, 0)),
        _resident(vecs.shape),
        _resident((d, dff)), _resident((d, dff)), _resident((dff, d)),
    ]
    out_specs = [pl.BlockSpec((tm, d), lambda i: (i, 0))]
    out_shape = [jax.ShapeDtypeStruct((m, d), F32)]
    args = [h, vecs, wg, wu, wd]
    attn_tokens = 0
    if attn is not None:
        qv, k, v = attn
        n_tok, n_mem, cs, lanes = k.shape
        assert n_tok % steps == 0
        attn_tokens = n_tok // steps
        chunk_spec = pl.BlockSpec((attn_tokens, cs, lanes), lambda i: (i, 0, 0))
        cache_spec = pl.BlockSpec((attn_tokens, n_mem, cs, lanes), lambda i: (i, 0, 0, 0))
        in_specs += [chunk_spec, cache_spec, cache_spec]
        out_specs.append(chunk_spec)
        out_shape.append(jax.ShapeDtypeStruct((n_tok, cs, lanes), F32))
        args += [qv, k, v]
    return pl.pallas_call(
        functools.partial(_ffn_kernel, attn_tokens=attn_tokens),
        grid=(steps,),
        in_specs=in_specs,
        out_specs=out_specs,
        out_shape=out_shape,
        compiler_params=pltpu.CompilerParams(
            dimension_semantics=("arbitrary",), vmem_limit_bytes=56 * MIB),
        name="ffn_attn" if attn_tokens else "ffn",
    )(*args)


def _sample_state_update(pr_ref, sa_ref, sb_ref, caw_ref, cbw_ref, cbb_ref,
                         lng_ref, lnb_ref, prea_ref, preb_ref, na_ref, nb_ref, tb, d):
    a_b = pr_ref[:, 0:d]
    ua = pr_ref[:, d:2 * d] * pr_ref[:, 2 * d:3 * d]
    ub = pr_ref[:, 3 * d:4 * d] * _sigmoid(pr_ref[:, 4 * d:5 * d])

    conv_b = cbw_ref[K_B - 1:K_B, :] * ub + cbb_ref[...]
    for j in range(K_B - 1):
        conv_b = conv_b + cbw_ref[j:j + 1, :] * sb_ref[j]
    preb_ref[...] = _layernorm_swish(conv_b, lng_ref[...], lnb_ref[...])
    for j in range(K_B - 2):
        nb_ref[j] = sb_ref[j + 1]
    nb_ref[K_B - 2] = ub

    for i in range(tb):
        row = slice(i, i + 1)
        conv_a = (caw_ref[0:1, :] * sa_ref[i, 0:1, :] + caw_ref[1:2, :] * sa_ref[i, 1:2, :]
                  + caw_ref[2:3, :] * ua[row])
        prea_ref[row, :] = a_b[row] * conv_a
        na_ref[i, 0:1, :] = sa_ref[i, 1:2, :]
        na_ref[i, 1:2, :] = ua[row]


def _to_chunks(a, lanes):
    *lead, nh, dh = a.shape
    a = a.reshape(*lead, nh, dh // lanes, lanes)
    a = jnp.swapaxes(a, -3, -2)
    return a.reshape(*lead, (dh // lanes) * nh, lanes)


def _from_chunks(a, nh):
    *lead, cs, lanes = a.shape
    a = a.reshape(*lead, cs // nh, nh, lanes)
    a = jnp.swapaxes(a, -3, -2)
    return a.reshape(*lead, nh, (cs // nh) * lanes)


def _from_tiles(a):
    rows, cs, lanes = a.shape
    flat = a.reshape(rows, cs * lanes)
    blocks = [flat[:, (half * N_HEADS + hd) * lanes:(half * N_HEADS + hd + 1) * lanes]
              for hd in range(N_HEADS) for half in range(cs // N_HEADS)]
    return jnp.concatenate(blocks, axis=-1)


def _stail_kernel(x_ref, prea_ref, preb_ref, o_ref, gl_ref, vec_ref,
                  wout_ref, wg_ref, wu_ref, wd_ref, y_ref, *, d):
    bgate_ref = _vec(vec_ref, "bgate")
    nffn_ref, nfin_ref = _vec(vec_ref, "nffn"), _vec(vec_ref, "nfin")

    def gate(i):
        return _sigmoid(gl_ref[:, i * d:(i + 1) * d] + bgate_ref[i:i + 1, :])

    ya = _dot(prea_ref[...].astype(BF16), wout_ref[0])
    yb = _dot(preb_ref[...].astype(BF16), wout_ref[1])
    yx = _dot(_from_tiles(o_ref[...]).astype(BF16), wout_ref[2])
    merged = gate(0) * ya + gate(1) * yb + gate(2) * yx
    h = x_ref[...] + _dot(merged.astype(BF16), wout_ref[3])

    hn = _rms(h, nffn_ref[...]).astype(BF16)
    g = _dot(hn, wg_ref[...])
    up = _dot(hn, wu_ref[...])
    act = (g * _sigmoid(g) * up).astype(BF16)
    h2 = h + _dot(act, wd_ref[...])
    y_ref[:, 0, :] = _rms(h2, nfin_ref[...])


def _stail_call(x, prea, preb, o_tiles, proj, vecs, wout, wg, wu, wd):
    m, d = x.shape
    kernel = functools.partial(_stail_kernel, d=d)
    full = pl.BlockSpec((m, d), lambda i: (0, 0))
    return pl.pallas_call(
        kernel,
        grid=(1,),
        in_specs=[
            full, full, full,
            pl.BlockSpec(o_tiles.shape, lambda i: (0, 0, 0)),
            pl.BlockSpec((m, 3 * d), lambda i: (0, 2)),
            _resident(vecs.shape),
            _resident(wout.shape),
            _resident(wg.shape), _resident(wu.shape), _resident(wd.shape),
        ],
        out_specs=pl.BlockSpec((m, 1, d), lambda i: (0, 0, 0)),
        out_shape=jax.ShapeDtypeStruct((m, 1, d), F32),
        compiler_params=pltpu.CompilerParams(
            dimension_semantics=("arbitrary",), vmem_limit_bytes=48 * MIB),
        name="sample_tail",
    )(x, prea, preb, o_tiles, proj, vecs, wout, wg, wu, wd)


def kernel(x_prompt, x_sample, mem_prompt, cache_mem_k, cache_mem_v, state_conv_a, state_conv_b, norm_mix, w_in, b_gate, conv_a_w, w_out_a, conv_b_w, conv_b_bias, ln_b_g, ln_b_b, w_out_b, norm_mem, w_k, w_v, w_out_x, w_o, norm_ffn, w_ff_gate, w_ff_up, w_ff_down, norm_final):
    depth = w_in.shape[0]
    assert depth == 1, "single-layer step only"
    b, t, d = x_prompt.shape
    sb_, st_, _ = x_sample.shape
    assert st_ == 1
    n_mem = mem_prompt.shape[1]
    nh, dh = w_k.shape[2], w_k.shape[3]
    assert nh == N_HEADS and nh * dh == d
    assert dh == 2 * LANES and 2 * nh == SUBLANES

    wk, wv = _to_chunks(w_k[0], LANES), _to_chunks(w_v[0], LANES)
    vecs = _pack_vectors(
        d, nmix=norm_mix[0], nmem=norm_mem[0], nffn=norm_ffn[0], nfin=norm_final,
        cbb=conv_b_bias[0], lng=ln_b_g[0], lnb=ln_b_b[0], bgate=b_gate[0],
        caw=conv_a_w[0], cbw=conv_b_w[0])

    xs = x_sample.reshape(sb_, d)
    (k_p, v_p, kt, vb), (proj_s, w_in_b, q_s), wout = _prep_call(
        mem_prompt, vecs, wk, wv, xs, w_in[0],
        [w_out_a[0], w_out_b[0], w_out_x[0], w_o[0]], q_block=5)

    (h_p, ca_p, cb_p), (wg, wu, wd), (prea, preb, ca_s, cb_s) = _mixer_call(
        x_prompt, kt, vb, w_in_b, wout, vecs,
        [w_ff_gate[0], w_ff_up[0], w_ff_down[0]],
        (proj_s, state_conv_a[0], jnp.swapaxes(state_conv_b[0], 0, 1)))

    y_p, o_s = _ffn_call(
        h_p.reshape(b * t, d), vecs, wg, wu, wd, PROMPT_TILE,
        attn=(q_s, _to_chunks(cache_mem_k[0], LANES), _to_chunks(cache_mem_v[0], LANES)))

    y_s = _stail_call(xs, prea, preb, o_s, proj_s, vecs, wout, wg, wu, wd)

    return (y_p.reshape(b, t, d), y_s,
            _from_chunks(k_p, nh)[None], _from_chunks(v_p, nh)[None],
            ca_p[None], cb_p[None], ca_s[None], jnp.swapaxes(cb_s, 0, 1)[None])
```
